```python
import jax, jax.numpy as jnp
from jax import lax
import numpy as np

D_MODEL = 1024
BATCH = 8
SEQ = 4096
DEPTH = 1

GRID_W = 64
CTX_LEN = 256
D_MIX = D_MODEL
D_REC = D_MIX // 2
D_ATT = D_MIX - D_REC
HEAD_DIM = 64
N_ATT_HEADS = D_ATT // HEAD_DIM
N_REC_BLOCKS = 8
REC_BLOCK = D_REC // N_REC_BLOCKS
CONV_W = 4
CONV_PAD_LEFT = 2
LRU_C = 8.0
WIN_ROWS = 8
WIN_COLS = 16
D_FF = 2816
N_MOD = 9
RES_W_FFN = 0.5
EPS = 1e-6
NEG_INF = -1e30
C_RG = D_REC
C_Q = 2 * D_REC
C_K = C_Q + D_ATT
C_V = C_K + D_ATT
D_IN = C_V + D_ATT

kernel_name = "hybrid_rglru_natten_macaron_dit_layer"


def rmsnorm(x, g):
    xf = x.astype(jnp.float32)
    y = xf * lax.rsqrt(jnp.mean(xf * xf, axis=-1, keepdims=True) + EPS)
    return (y * g.astype(jnp.float32)).astype(x.dtype)


def modulated_norm(x, g, mod, idx):
    shift = mod[:, 3 * idx][:, None]
    scale = mod[:, 3 * idx + 1][:, None]
    return rmsnorm(x, g) * (1.0 + scale) + shift


def gated_residual(x, y, g, mod, idx, res_w):
    gate = mod[:, 3 * idx + 2][:, None]
    return x + res_w * gate * rmsnorm(y, g)


def sublayer(x, mod, idx, g_pre, g_post, fn, res_w):
    return gated_residual(x, fn(modulated_norm(x, g_pre, mod, idx)), g_post, mod, idx, res_w)


def swiglu(h, w_gu, w_down):
    g, u = jnp.split(h @ w_gu, 2, axis=-1)
    return (jax.nn.silu(g) * u) @ w_down


def centred_dwconv(x, w, b):
    y = lax.conv_general_dilated(
        x, w[:, None, :], window_strides=(1,),
        padding=[(CONV_PAD_LEFT, CONV_W - 1 - CONV_PAD_LEFT)],
        dimension_numbers=("NWC", "WIO", "NWC"), feature_group_count=x.shape[-1])
    return y + b


def linear_scan(a, u, h0, reverse):
    def combine(left, right):
        a_l, b_l = left
        a_r, b_r = right
        return a_l * a_r, a_r * b_l + b_r
    a_cum, b_cum = lax.associative_scan(combine, (a, u), reverse=reverse, axis=1)
    return a_cum * h0[:, None] + b_cum


def rglru_direction(x, wa, ba, wx, bx, lam, h0, reverse):
    b, t, _ = x.shape
    xb = x.reshape(b, t, N_REC_BLOCKS, REC_BLOCK)
    gate_a = jnp.einsum("btnc,ncd->btnd", xb, wa).reshape(b, t, D_REC) + ba
    gate_x = jnp.einsum("btnc,ncd->btnd", xb, wx).reshape(b, t, D_REC) + bx
    log_a = LRU_C * jax.nn.sigmoid(gate_a.astype(jnp.float32)) * jax.nn.log_sigmoid(lam.astype(jnp.float32))
    u = jnp.sqrt(-jnp.expm1(2.0 * log_a)) * jax.nn.sigmoid(gate_x.astype(jnp.float32)) * x.astype(jnp.float32)
    h = linear_scan(jnp.exp(log_a), u, h0, reverse)
    final = h[:, 0] if reverse else h[:, -1]
    return h, final


def rglru_group(xr, xr_ctx, conv_w, conv_b, wa, ba, wx, bx, lam, need_ctx_out):
    xl = centred_dwconv(xr, conv_w, conv_b)
    xc = centred_dwconv(xr_ctx, conv_w, conv_b)
    h0 = jnp.zeros((xc.shape[0], D_REC), jnp.float32)
    ys_lat, ys_ctx = [], []
    for d, rev in enumerate((False, True)):
        hc, h_fin = rglru_direction(xc, wa[d], ba[d], wx[d], bx[d], lam[d], h0, rev)
        hl, _ = rglru_direction(xl, wa[d], ba[d], wx[d], bx[d], lam[d], h_fin, rev)
        ys_lat.append(hl)
        ys_ctx.append(hc)
    y_lat = (ys_lat[0] + ys_lat[1]).astype(xr.dtype)
    y_ctx = (ys_ctx[0] + ys_ctx[1]).astype(xr.dtype) if need_ctx_out else None
    return y_lat, y_ctx


def neighbourhood_attention(q, k, v, k_ctx, v_ctx, rpb):
    b, l = q.shape[:2]
    rows = l // GRID_W
    win_r = min(WIN_ROWS, rows)
    scale = HEAD_DIM ** -0.5

    def to_grid(t):
        return t.reshape(b, rows, GRID_W, N_ATT_HEADS, HEAD_DIM).transpose(0, 3, 1, 2, 4)

    qg, kg, vg = to_grid(q), to_grid(k), to_grid(v)
    col = np.arange(GRID_W)
    col_start = np.clip(col - WIN_COLS // 2, 0, GRID_W - WIN_COLS)
    col_valid = (col[None, :] >= col_start[:, None]) & (col[None, :] < col_start[:, None] + WIN_COLS)
    col_off = np.clip(col[None, :] - col[:, None] + WIN_COLS - 1, 0, 2 * WIN_COLS - 2)
    rpb_cols = rpb.astype(jnp.float32)[:, :, col_off]

    def one_row(args):
        r, q_row = args
        start = jnp.clip(r - win_r // 2, 0, rows - win_r)
        k_band = lax.dynamic_slice_in_dim(kg, start, win_r, axis=2)
        v_band = lax.dynamic_slice_in_dim(vg, start, win_r, axis=2)
        s_loc = jnp.einsum("bhqd,bhjkd->bhqjk", q_row, k_band).astype(jnp.float32) * scale
        row_off = start + jnp.arange(win_r) - r + WIN_ROWS - 1
        bias = jnp.take(rpb_cols, row_off, axis=1).transpose(0, 2, 1, 3)
        s_loc = jnp.where(col_valid[:, None, :], s_loc + bias, NEG_INF)
        s_ctx = jnp.einsum("bhqd,bhkd->bhqk", q_row, k_ctx).astype(jnp.float32) * scale
        s = jnp.concatenate([s_loc.reshape(b, N_ATT_HEADS, GRID_W, win_r * GRID_W), s_ctx], axis=-1)
        p = jax.nn.softmax(s, axis=-1)
        p_loc = p[..., :win_r * GRID_W].reshape(b, N_ATT_HEADS, GRID_W, win_r, GRID_W).astype(v.dtype)
        p_ctx = p[..., win_r * GRID_W:].astype(v.dtype)
        return (jnp.einsum("bhqjk,bhjkd->bhqd", p_loc, v_band)
                + jnp.einsum("bhqk,bhkd->bhqd", p_ctx, v_ctx))

    out = lax.map(one_row, (jnp.arange(rows), qg.transpose(2, 0, 1, 3, 4)))
    return out.transpose(1, 0, 3, 2, 4).reshape(b, l, D_ATT)


def context_attention(q_ctx, k_ctx, v_ctx):
    s = jnp.einsum("bhqd,bhkd->bhqk", q_ctx, k_ctx).astype(jnp.float32) * HEAD_DIM ** -0.5
    p = jax.nn.softmax(s, axis=-1).astype(v_ctx.dtype)
    o = jnp.einsum("bhqk,bhkd->bhqd", p, v_ctx)
    b, _, t, _ = o.shape
    return o.transpose(0, 2, 1, 3).reshape(b, t, D_ATT)


def split_heads(t):
    return t.reshape(t.shape[0], t.shape[1], N_ATT_HEADS, HEAD_DIM)


def mixer(h, hc, w_in, w_out, conv_w, conv_b, wa, ba, wx, bx, lam, rpb, need_ctx_out):
    p = h @ w_in
    xr, gr, q, k, v = p[..., :C_RG], p[..., C_RG:C_Q], p[..., C_Q:C_K], p[..., C_K:C_V], p[..., C_V:]
    xr_ctx = hc @ w_in[:, :C_RG]
    kv_ctx = hc @ w_in[:, C_K:]
    k_ctx = split_heads(kv_ctx[..., :D_ATT]).transpose(0, 2, 1, 3)
    v_ctx = split_heads(kv_ctx[..., D_ATT:]).transpose(0, 2, 1, 3)

    y_rec, y_rec_ctx = rglru_group(xr, xr_ctx, conv_w, conv_b, wa, ba, wx, bx, lam, need_ctx_out)
    y_att = neighbourhood_attention(split_heads(q), split_heads(k), split_heads(v), k_ctx, v_ctx, rpb)
    y = jnp.concatenate([y_rec * jax.nn.gelu(gr), y_att], axis=-1) @ w_out
    if not need_ctx_out:
        return y, None
    g_ctx = hc @ w_in[:, C_RG:C_Q]
    q_ctx = split_heads(hc @ w_in[:, C_Q:C_K]).transpose(0, 2, 1, 3)
    y_att_ctx = context_attention(q_ctx, k_ctx, v_ctx)
    yc = jnp.concatenate([y_rec_ctx * jax.nn.gelu(g_ctx), y_att_ctx], axis=-1) @ w_out
    return y, yc


def setup_inputs(seed: int = 0) -> dict:
    key = jax.random.key(seed)
    ks = jax.random.split(key, 24)
    nrm = jax.random.normal
    f32 = jnp.float32
    lam_u = jax.random.uniform(ks[20], (DEPTH, 2, D_REC), f32, 0.9, 0.999)
    a0 = lam_u ** (1.0 / LRU_C)
    return {
        "x": nrm(ks[0], (BATCH, SEQ, D_MODEL), f32),
        "c": nrm(ks[1], (BATCH, D_MODEL), f32),
        "ctx": nrm(ks[2], (BATCH, CTX_LEN, D_MODEL), f32),
        "c_ctx": nrm(ks[3], (D_MODEL,), f32),
        "w_mod": nrm(ks[4], (DEPTH, D_MODEL, N_MOD * D_MODEL), f32) * D_MODEL ** -0.5,
        "b_mod": nrm(ks[5], (DEPTH, N_MOD * D_MODEL), f32) * 0.02,
        "norm_pre": 1.0 + 0.05 * nrm(ks[6], (DEPTH, 3, D_MODEL), f32),
        "norm_post": 1.0 + 0.05 * nrm(ks[7], (DEPTH, 3, D_MODEL), f32),
        "ffn1_w_gu": nrm(ks[8], (DEPTH, D_MODEL, 2 * D_FF), f32) * D_MODEL ** -0.5,
        "ffn1_w_down": nrm(ks[9], (DEPTH, D_FF, D_MODEL), f32) * D_FF ** -0.5,
        "ffn2_w_gu": nrm(ks[10], (DEPTH, D_MODEL, 2 * D_FF), f32) * D_MODEL ** -0.5,
        "ffn2_w_down": nrm(ks[11], (DEPTH, D_FF, D_MODEL), f32) * D_FF ** -0.5,
        "w_in": nrm(ks[12], (DEPTH, D_MODEL, D_IN), f32) * D_MODEL ** -0.5,
        "w_out": nrm(ks[13], (DEPTH, D_MIX, D_MODEL), f32) * D_MIX ** -0.5,
        "conv_w": nrm(ks[14], (DEPTH, CONV_W, D_REC), f32) * CONV_W ** -0.5,
        "conv_b": nrm(ks[15], (DEPTH, D_REC), f32) * 0.02,
        "lru_wa": nrm(ks[16], (DEPTH, 2, N_REC_BLOCKS, REC_BLOCK, REC_BLOCK), f32) * REC_BLOCK ** -0.5,
        "lru_ba": nrm(ks[17], (DEPTH, 2, D_REC), f32) * 0.02,
        "lru_wx": nrm(ks[18], (DEPTH, 2, N_REC_BLOCKS, REC_BLOCK, REC_BLOCK), f32) * REC_BLOCK ** -0.5,
        "lru_bx": nrm(ks[19], (DEPTH, 2, D_REC), f32) * 0.02,
        "lru_lambda": jnp.log(a0) - jnp.log1p(-a0),
        "na_rpb": nrm(ks[21], (DEPTH, N_ATT_HEADS, 2 * WIN_ROWS - 1, 2 * WIN_COLS - 1), f32) * 0.1,
    }


def reference(x, c, ctx, c_ctx, w_mod, b_mod, norm_pre, norm_post, ffn1_w_gu, ffn1_w_down,
              ffn2_w_gu, ffn2_w_down, w_in, w_out, conv_w, conv_b, lru_wa, lru_ba, lru_wx,
              lru_bx, lru_lambda, na_rpb):
    b = x.shape[0]
    xc = ctx
    for l in range(DEPTH):
        last = l == DEPTH - 1
        mod = (jax.nn.silu(c) @ w_mod[l] + b_mod[l]).reshape(b, N_MOD, D_MODEL)
        mod_c = (jax.nn.silu(c_ctx)[None] @ w_mod[l] + b_mod[l]).reshape(1, N_MOD, D_MODEL)
        ffn1 = lambda h: swiglu(h, ffn1_w_gu[l], ffn1_w_down[l])
        ffn2 = lambda h: swiglu(h, ffn2_w_gu[l], ffn2_w_down[l])

        x = sublayer(x, mod, 0, norm_pre[l, 0], norm_post[l, 0], ffn1, RES_W_FFN)
        xc = sublayer(xc, mod_c, 0, norm_pre[l, 0], norm_post[l, 0], ffn1, RES_W_FFN)

        h = modulated_norm(x, norm_pre[l, 1], mod, 1)
        hc = modulated_norm(xc, norm_pre[l, 1], mod_c, 1)
        y, yc = mixer(h, hc, w_in[l], w_out[l], conv_w[l], conv_b[l], lru_wa[l], lru_ba[l],
                      lru_wx[l], lru_bx[l], lru_lambda[l], na_rpb[l], need_ctx_out=not last)
        x = gated_residual(x, y, norm_post[l, 1], mod, 1, 1.0)
        if not last:
            xc = gated_residual(xc, yc, norm_post[l, 1], mod_c, 1, 1.0)

        x = sublayer(x, mod, 2, norm_pre[l, 2], norm_post[l, 2], ffn2, RES_W_FFN)
        if not last:
            xc = sublayer(xc, mod_c, 2, norm_pre[l, 2], norm_post[l, 2], ffn2, RES_W_FFN)
    return x
```

```python
import functools

import numpy as np
import jax
import jax.numpy as jnp
from jax import lax
from jax.experimental import pallas as pl
from jax.experimental.pallas import tpu as pltpu

F32 = jnp.float32
BF16 = jnp.bfloat16

D_MODEL = 1024
D_REC = 512
D_ATT = 512
HEAD_DIM = 64
N_ATT_HEADS = 8
N_REC_BLOCKS = 8
REC_BLOCK = 64
CONV_W = 4
CONV_PAD_LEFT = 2
LRU_C = 8.0
GRID_W = 64
WIN_ROWS = 8
WIN_COLS = 16
D_FF = 2816
N_MOD = 9
EPS = 1e-6
NEG_INF = -1e30
C_RG, C_Q, C_K, C_V = 512, 1024, 1536, 2048
D_IN = 2560

LANES = 128
SUBLANES = 8
VMEM_LIMIT_BYTES = 56 * 1024 * 1024

TOKEN_TILE = 512
FF_CHUNK = 256
MOD_ROWS = 16
REC_GROUP = LANES
GATE_CHUNK = 512
SCAN_UNROLL = 4
ATT_Q_ROWS = 4
ATT_BAND_ROWS = 12
HEAD_PAIR = LANES // HEAD_DIM


def _sigmoid(x):
    return 0.5 * (jnp.tanh(0.5 * x) + 1.0)


def _rms(x, g):
    return x * lax.rsqrt(jnp.mean(x * x, axis=-1, keepdims=True) + EPS) * g


def _dot(a, b):
    return jnp.dot(a, b, preferred_element_type=F32)


def _resident(shape):
    return pl.BlockSpec(shape, lambda *_: (0,) * len(shape), pipeline_mode=pl.Buffered(1))


def _mod_kernel(c_ref, w_ref, b_ref, o_ref):
    c = c_ref[...]
    s = (c * _sigmoid(c)).astype(BF16)
    o_ref[...] = _dot(s, w_ref[...].astype(BF16)) + b_ref[...]


def _modulation(c_rows, w_mod, b_mod):
    n = w_mod.shape[1]
    return pl.pallas_call(
        _mod_kernel,
        grid=(n // D_MODEL,),
        in_specs=[
            pl.BlockSpec((MOD_ROWS, D_MODEL), lambda j: (0, 0)),
            pl.BlockSpec((D_MODEL, D_MODEL), lambda j: (0, j)),
            pl.BlockSpec((1, D_MODEL), lambda j: (0, j)),
        ],
        out_specs=pl.BlockSpec((MOD_ROWS, D_MODEL), lambda j: (0, j)),
        out_shape=jax.ShapeDtypeStruct((MOD_ROWS, n), F32),
        name="mod",
    )(c_rows, w_mod, b_mod.reshape(1, n))


def _swiglu(h, wgu_ref, wd_ref):
    acc = None
    for c in range(D_FF // FF_CHUNK):
        lo = c * FF_CHUNK
        g = _dot(h, wgu_ref[:, lo:lo + FF_CHUNK])
        u = _dot(h, wgu_ref[:, D_FF + lo:D_FF + lo + FF_CHUNK])
        a = (g * _sigmoid(g) * u).astype(BF16)
        d = _dot(a, wd_ref[lo:lo + FF_CHUNK, :])
        acc = d if acc is None else acc + d
    return acc


def _ffn_sublayer(x, mod_ref, npre_ref, npost_ref, idx, wgu_ref, wd_ref):
    shift = mod_ref[0, 3 * idx:3 * idx + 1, :]
    scale = mod_ref[0, 3 * idx + 1:3 * idx + 2, :]
    gate = mod_ref[0, 3 * idx + 2:3 * idx + 3, :]
    h = _rms(x, npre_ref[idx:idx + 1, :]) * (1.0 + scale) + shift
    y = _swiglu(h.astype(BF16), wgu_ref, wd_ref)
    return x + 0.5 * gate * _rms(y, npost_ref[idx:idx + 1, :])


def _layer_in_kernel(x_ref, mod_ref, npre_ref, npost_ref, wgu_ref, wd_ref, win_ref, *out_refs,
                     proj_cols, write_x):
    x1 = _ffn_sublayer(x_ref[0], mod_ref, npre_ref, npost_ref, 0, wgu_ref, wd_ref)
    outs = list(out_refs)
    if write_x:
        outs.pop(0)[0] = x1
    shift = mod_ref[0, 3:4, :]
    scale = mod_ref[0, 4:5, :]
    h = (_rms(x1, npre_ref[1:2, :]) * (1.0 + scale) + shift).astype(BF16)
    for o_ref, (lo, width, mult) in zip(outs, proj_cols):
        p = _dot(h, win_ref[:, lo:lo + width])
        if mult != 1.0:
            p = p * mult
        o_ref[0] = p.astype(o_ref.dtype)


def _layer_in(x, mod, npre, npost, wgu, wd, win, proj, write_x):
    b, t, d = x.shape
    tm = min(TOKEN_TILE, t)
    tok = lambda w: pl.BlockSpec((1, tm, w), lambda i, j: (i, j, 0))
    out_shape, out_specs = [], []
    if write_x:
        out_shape.append(jax.ShapeDtypeStruct((b, t, d), F32))
        out_specs.append(tok(d))
    for _, width, _, dtype in proj:
        out_shape.append(jax.ShapeDtypeStruct((b, t, width), dtype))
        out_specs.append(tok(width))
    return pl.pallas_call(
        functools.partial(_layer_in_kernel, proj_cols=tuple(p[:3] for p in proj), write_x=write_x),
        grid=(b, t // tm),
        in_specs=[
            tok(d),
            pl.BlockSpec((1, N_MOD, d), lambda i, j: (i, 0, 0)),
            _resident(npre.shape), _resident(npost.shape),
            _resident(wgu.shape), _resident(wd.shape), _resident(win.shape),
        ],
        out_specs=out_specs,
        out_shape=out_shape,
        compiler_params=pltpu.CompilerParams(
            dimension_semantics=("parallel", "parallel"), vmem_limit_bytes=VMEM_LIMIT_BYTES),
        name="layer_in",
    )(x, mod, npre, npost, wgu, wd, win)


def _layer_out_kernel(x_ref, yr_ref, ya_ref, mod_ref, npre_ref, npost_ref, wout_ref, wgu_ref, wd_ref,
                      o_ref):
    y = _dot(yr_ref[0], wout_ref[:D_REC, :]) + _dot(ya_ref[0], wout_ref[D_REC:, :])
    x2 = x_ref[0] + mod_ref[0, 5:6, :] * _rms(y, npost_ref[1:2, :])
    o_ref[0] = _ffn_sublayer(x2, mod_ref, npre_ref, npost_ref, 2, wgu_ref, wd_ref)


def _layer_out(x1, y_rec, y_att, mod, npre, npost, wout, wgu, wd):
    b, t, d = x1.shape
    tm = min(TOKEN_TILE, t)
    tok = lambda w: pl.BlockSpec((1, tm, w), lambda i, j: (i, j, 0))
    return pl.pallas_call(
        _layer_out_kernel,
        grid=(b, t // tm),
        in_specs=[
            tok(d), tok(D_REC), tok(D_ATT),
            pl.BlockSpec((1, N_MOD, d), lambda i, j: (i, 0, 0)),
            _resident(npre.shape), _resident(npost.shape),
            _resident(wout.shape), _resident(wgu.shape), _resident(wd.shape),
        ],
        out_specs=tok(d),
        out_shape=jax.ShapeDtypeStruct((b, t, d), F32),
        compiler_params=pltpu.CompilerParams(
            dimension_semantics=("parallel", "parallel"), vmem_limit_bytes=VMEM_LIMIT_BYTES),
        name="layer_out",
    )(x1, y_rec, y_att, mod, npre, npost, wout, wgu, wd)


def _scan_tile(a, u, carry, row, reverse):
    for s in (1, 2, 4):
        if reverse:
            keep = row < SUBLANES - s
            shift = SUBLANES - s
        else:
            keep = row >= s
            shift = s
        a_sh = jnp.where(keep, pltpu.roll(a, shift, 0), 1.0)
        u_sh = jnp.where(keep, pltpu.roll(u, shift, 0), 0.0)
        u = a * u_sh + u
        a = a * a_sh
    h = a * carry + u
    return h, (h[0:1, :] if reverse else h[SUBLANES - 1:SUBLANES, :])


def _rglru_kernel(xr_ref, xc_ref, gr_ref, cw_ref, cb_ref, wg_ref, bg_ref, lam_ref, o_ref,
                  pad_s, conv_s, af_s, uf_s, ab_s, ub_s):
    cg = REC_GROUP
    lam = lam_ref[0]
    log_sig_lam = jnp.minimum(lam, 0.0) - jnp.log(1.0 + jnp.exp(-jnp.abs(lam)))
    row = lax.broadcasted_iota(jnp.int32, (SUBLANES, cg), 0)

    def conv_gates(src_ref, n):
        halo = SUBLANES
        pad_s[0:halo, :] = jnp.zeros((halo, cg), F32)
        pad_s[halo:halo + n, :] = src_ref[0]
        pad_s[halo + n:2 * halo + n, :] = jnp.zeros((halo, cg), F32)
        y = cb_ref[...]
        for k in range(CONV_W):
            off = halo + k - CONV_PAD_LEFT
            y = y + pad_s[off:off + n, :] * cw_ref[k:k + 1, :]
        conv_s[0:n, :] = y

        chunk = min(GATE_CHUNK, n)

        def gate_body(ci, _):
            r0 = pl.multiple_of(ci * chunk, chunk)
            xv = conv_s[pl.ds(r0, chunk), :]
            g = _dot(xv.astype(BF16), wg_ref[0]) + bg_ref[0]
            for d, (a_s, u_s) in enumerate(((af_s, uf_s), (ab_s, ub_s))):
                ga = g[:, (2 * d) * cg:(2 * d + 1) * cg]
                gx = g[:, (2 * d + 1) * cg:(2 * d + 2) * cg]
                log_a = LRU_C * _sigmoid(ga) * log_sig_lam[:, d * cg:(d + 1) * cg]
                a = jnp.exp(log_a)
                a_s[pl.ds(r0, chunk), :] = a
                one_minus_a2 = -jnp.tanh(log_a) * (a * a + 1.0)
                u_s[pl.ds(r0, chunk), :] = jnp.sqrt(one_minus_a2) * _sigmoid(gx) * xv
            return 0

        lax.fori_loop(0, n // chunk, gate_body, 0)

    def scan(n, hf0, hb0):
        tiles = n // SUBLANES

        def body(k, carry):
            hf, hb = carry
            rf = pl.multiple_of(k * SUBLANES, SUBLANES)
            rb = pl.multiple_of((tiles - 1 - k) * SUBLANES, SUBLANES)
            h, hf = _scan_tile(af_s[pl.ds(rf, SUBLANES), :], uf_s[pl.ds(rf, SUBLANES), :], hf, row, False)
            uf_s[pl.ds(rf, SUBLANES), :] = h
            h, hb = _scan_tile(ab_s[pl.ds(rb, SUBLANES), :], ub_s[pl.ds(rb, SUBLANES), :], hb, row, True)
            ub_s[pl.ds(rb, SUBLANES), :] = h
            return hf, hb

        return lax.fori_loop(0, tiles, body, (hf0, hb0), unroll=SCAN_UNROLL)

    n_ctx = xc_ref.shape[1]
    n_lat = xr_ref.shape[1]
    zero = jnp.zeros((1, cg), F32)
    conv_gates(xc_ref, n_ctx)
    hf, hb = scan(n_ctx, zero, zero)
    conv_gates(xr_ref, n_lat)
    scan(n_lat, hf, hb)
    y = uf_s[0:n_lat, :] + ub_s[0:n_lat, :]
    o_ref[0] = (y * jax.nn.gelu(gr_ref[0])).astype(o_ref.dtype)


def _rglru(xr, xr_ctx, gr, conv_w, conv_b, wg, bg, lam):
    b, t, _ = xr.shape
    n_ctx = xr_ctx.shape[1]
    cg = REC_GROUP
    seq = lambda n: pl.BlockSpec((1, n, cg), lambda i, j: (i, 0, j))
    grp = lambda shape: pl.BlockSpec((1,) + shape, lambda i, j: (j, 0, 0))
    return pl.pallas_call(
        _rglru_kernel,
        grid=(b, D_REC // cg),
        in_specs=[
            seq(t), seq(n_ctx), seq(t),
            pl.BlockSpec((CONV_W, cg), lambda i, j: (0, j)),
            pl.BlockSpec((1, cg), lambda i, j: (0, j)),
            grp((cg, 4 * cg)), grp((1, 4 * cg)), grp((1, 2 * cg)),
        ],
        out_specs=seq(t),
        out_shape=jax.ShapeDtypeStruct((b, t, D_REC), BF16),
        scratch_shapes=[pltpu.VMEM((t + 2 * SUBLANES, cg), F32)] + [pltpu.VMEM((t, cg), F32)] * 5,
        compiler_params=pltpu.CompilerParams(
            dimension_semantics=("parallel", "parallel"), vmem_limit_bytes=VMEM_LIMIT_BYTES),
        name="rglru",
    )(xr, xr_ctx, gr, conv_w, conv_b.reshape(1, D_REC), wg, bg, lam)


def _rglru_params(lru_wa, lru_ba, lru_wx, lru_bx, lru_lambda):
    cg = REC_GROUP
    n_grp = D_REC // cg
    per = cg // REC_BLOCK

    def block_diag(w):
        w = w.reshape(n_grp, per, REC_BLOCK, REC_BLOCK)
        eye = jnp.eye(per, dtype=w.dtype)
        return jnp.einsum("gpcd,pq->gpcqd", w, eye).reshape(n_grp, cg, cg)

    wg = jnp.concatenate([block_diag(lru_wa[0]), block_diag(lru_wx[0]),
                          block_diag(lru_wa[1]), block_diag(lru_wx[1])], axis=-1).astype(BF16)
    vec = lambda v: v.reshape(n_grp, 1, cg)
    bg = jnp.concatenate([vec(lru_ba[0]), vec(lru_bx[0]), vec(lru_ba[1]), vec(lru_bx[1])], axis=-1)
    lam = jnp.concatenate([vec(lru_lambda[0]), vec(lru_lambda[1])], axis=-1)
    return wg, bg, lam


def _attn_kernel(q_ref, k_ref, v_ref, kc_ref, vc_ref, tbl_ref, o_ref, *, rows):
    i = pl.program_id(2)
    n_blk = pl.num_programs(2)
    start = jnp.clip(ATT_Q_ROWS * i - WIN_ROWS // 2, 0, rows - ATT_BAND_ROWS)
    kind = jnp.where(i == 0, 0, jnp.where(i == n_blk - 1, 2, 1))
    off = pl.multiple_of(start * GRID_W, GRID_W)
    band = ATT_BAND_ROWS * GRID_W
    kb = k_ref[0, pl.ds(off, band), :]
    vb = v_ref[0, pl.ds(off, band), :]
    kc = kc_ref[0]
    vc = vc_ref[0]
    q = q_ref[0]
    lane = lax.broadcasted_iota(jnp.int32, q.shape, 1)
    contract_last = (((1,), (1,)), ((), ()))
    outs = []
    for hh in range(HEAD_PAIR):
        mine = (lane >= hh * HEAD_DIM) & (lane < (hh + 1) * HEAD_DIM)
        qm = jnp.where(mine, q, jnp.zeros_like(q))
        s_loc = lax.dot_general(qm, kb, contract_last, preferred_element_type=F32) + tbl_ref[kind, hh]
        s_ctx = lax.dot_general(qm, kc, contract_last, preferred_element_type=F32)
        m = jnp.maximum(jnp.max(s_loc, axis=-1, keepdims=True), jnp.max(s_ctx, axis=-1, keepdims=True))
        p_loc = jnp.exp(s_loc - m)
        p_ctx = jnp.exp(s_ctx - m)
        denom = jnp.sum(p_loc, axis=-1, keepdims=True) + jnp.sum(p_ctx, axis=-1, keepdims=True)
        o = _dot(p_loc.astype(BF16), vb) + _dot(p_ctx.astype(BF16), vc)
        outs.append(o / denom)
    o_ref[0] = jnp.where(lane < HEAD_DIM, outs[0], outs[1]).astype(o_ref.dtype)


def _attention(q, k, v, k_ctx, v_ctx, tbl):
    b, t, _ = q.shape
    n_ctx = k_ctx.shape[1]
    rows = t // GRID_W
    tq = ATT_Q_ROWS * GRID_W
    band = ATT_BAND_ROWS * GRID_W
    full = lambda n: pl.BlockSpec((1, n, LANES), lambda bi, p, i: (bi, 0, p))
    blk = pl.BlockSpec((1, tq, LANES), lambda bi, p, i: (bi, i, p))
    return pl.pallas_call(
        functools.partial(_attn_kernel, rows=rows),
        grid=(b, D_ATT // LANES, rows // ATT_Q_ROWS),
        in_specs=[
            blk, full(t), full(t), full(n_ctx), full(n_ctx),
            pl.BlockSpec((3, HEAD_PAIR, tq, band), lambda bi, p, i: (0, p, 0, 0)),
        ],
        out_specs=blk,
        out_shape=jax.ShapeDtypeStruct((b, t, D_ATT), BF16),
        compiler_params=pltpu.CompilerParams(
            dimension_semantics=("parallel", "parallel", "arbitrary"),
            vmem_limit_bytes=VMEM_LIMIT_BYTES),
        name="attn",
    )(q, k, v, k_ctx, v_ctx, tbl)


def _bias_tables(rpb, rows):
    n_blk = rows // ATT_Q_ROWS
    qi = np.arange(ATT_Q_ROWS * GRID_W)
    ki = np.arange(ATT_BAND_ROWS * GRID_W)
    q_row, q_col = qi // GRID_W, qi % GRID_W
    k_row, k_col = ki // GRID_W, ki % GRID_W
    col_start = np.clip(q_col - WIN_COLS // 2, 0, GRID_W - WIN_COLS)
    col_valid = (k_col[None, :] >= col_start[:, None]) & (k_col[None, :] < col_start[:, None] + WIN_COLS)
    col_off = np.clip(k_col[None, :] - q_col[:, None] + WIN_COLS - 1, 0, 2 * WIN_COLS - 2)
    tables = []
    for i in (0, 1, n_blk - 1):
        band_start = np.clip(ATT_Q_ROWS * i - WIN_ROWS // 2, 0, rows - ATT_BAND_ROWS)
        r = ATT_Q_ROWS * i + q_row
        kr = band_start + k_row
        win_start = np.clip(r - WIN_ROWS // 2, 0, rows - WIN_ROWS)
        row_valid = (kr[None, :] >= win_start[:, None]) & (kr[None, :] < win_start[:, None] + WIN_ROWS)
        row_off = np.clip(kr[None, :] - r[:, None] + WIN_ROWS - 1, 0, 2 * WIN_ROWS - 2)
        bias = rpb[:, row_off, col_off]
        tables.append(jnp.where((row_valid & col_valid)[None], bias, NEG_INF))
    return jnp.stack(tables)


def kernel(x, c, ctx, c_ctx, w_mod, b_mod, norm_pre, norm_post, ffn1_w_gu, ffn1_w_down, ffn2_w_gu,
           ffn2_w_down, w_in, w_out, conv_w, conv_b, lru_wa, lru_ba, lru_wx, lru_bx, lru_lambda, na_rpb):
    assert w_mod.shape[0] == 1, "single-layer problem"
    b, t, d = x.shape
    rows = t // GRID_W
    assert d == D_MODEL and t % (ATT_Q_ROWS * GRID_W) == 0 and rows >= 3 * ATT_Q_ROWS
    assert b + 1 <= MOD_ROWS

    c_rows = jnp.concatenate([c, c_ctx[None], jnp.zeros((MOD_ROWS - b - 1, d), F32)], axis=0)
    mod_all = _modulation(c_rows, w_mod[0], b_mod[0])
    mod = mod_all[:b].reshape(b, N_MOD, d)
    mod_ctx = jnp.broadcast_to(mod_all[b].reshape(1, N_MOD, d), (b, N_MOD, d))

    npre, npost = norm_pre[0], norm_post[0]
    wgu1, wd1 = ffn1_w_gu[0].astype(BF16), ffn1_w_down[0].astype(BF16)
    wgu2, wd2 = ffn2_w_gu[0].astype(BF16), ffn2_w_down[0].astype(BF16)
    win, wout = w_in[0].astype(BF16), w_out[0].astype(BF16)

    q_scale = HEAD_DIM ** -0.5
    x1, xr, gr, q, k, v = _layer_in(
        x, mod, npre, npost, wgu1, wd1, win,
        proj=((0, D_REC, 1.0, F32), (C_RG, D_REC, 1.0, F32), (C_Q, D_ATT, q_scale, BF16),
              (C_K, D_ATT, 1.0, BF16), (C_V, D_ATT, 1.0, BF16)),
        write_x=True)
    xr_ctx, k_ctx, v_ctx = _layer_in(
        ctx, mod_ctx, npre, npost, wgu1, wd1, win,
        proj=((0, D_REC, 1.0, F32), (C_K, D_ATT, 1.0, BF16), (C_V, D_ATT, 1.0, BF16)),
        write_x=False)

    wg, bg, lam = _rglru_params(lru_wa[0], lru_ba[0], lru_wx[0], lru_bx[0], lru_lambda[0])
    y_rec = _rglru(xr, xr_ctx, gr, conv_w[0], conv_b[0], wg, bg, lam)
    y_att = _attention(q, k, v, k_ctx, v_ctx, _bias_tables(na_rpb[0].astype(F32), rows))
    return _layer_out(x1, y_rec, y_att, mod, npre, npost, wout, wgu2, wd2)
```

```python
import functools

import numpy as np
import jax
import jax.numpy as jnp
from jax import lax
from jax.experimental import pallas as pl
from jax.experimental.pallas import tpu as pltpu

F32 = jnp.float32
BF16 = jnp.bfloat16

D_MODEL = 1024
D_REC = 512
D_ATT = 512
HEAD_DIM = 64
N_ATT_HEADS = 8
N_REC_BLOCKS = 8
REC_BLOCK = 64
CONV_W = 4
CONV_PAD_LEFT = 2
LRU_C = 8.0
GRID_W = 64
WIN_ROWS = 8
WIN_COLS = 16
D_FF = 2816
N_MOD = 9
EPS = 1e-6
NEG_INF = -1e30
C_RG, C_Q, C_K, C_V = 512, 1024, 1536, 2048
D_IN = 2560

LANES = 128
SUBLANES = 8
VMEM_LIMIT_BYTES = 56 * 1024 * 1024

TOKEN_TILE = 512
FF_CHUNK = 256
MOD_ROWS = 16
REC_GROUP = LANES
GATE_CHUNK = 512
SCAN_UNROLL = 4
ATT_Q_ROWS = 4
ATT_BAND_ROWS = 12
HEAD_PAIR = LANES // HEAD_DIM


def _sigmoid(x):
    return 0.5 * (jnp.tanh(0.5 * x) + 1.0)


def _rms(x, g):
    return x * lax.rsqrt(jnp.mean(x * x, axis=-1, keepdims=True) + EPS) * g


def _dot(a, b):
    return jnp.dot(a, b, preferred_element_type=F32)


def _resident(shape):
    return pl.BlockSpec(shape, lambda *_: (0,) * len(shape), pipeline_mode=pl.Buffered(1))


def _mod_kernel(c_ref, w_ref, b_ref, o_ref):
    c = c_ref[...]
    s = (c * _sigmoid(c)).astype(BF16)
    o_ref[...] = _dot(s, w_ref[...].astype(BF16)) + b_ref[...]


def _modulation(c_rows, w_mod, b_mod):
    n = w_mod.shape[1]
    return pl.pallas_call(
        _mod_kernel,
        grid=(n // D_MODEL,),
        in_specs=[
            pl.BlockSpec((MOD_ROWS, D_MODEL), lambda j: (0, 0)),
            pl.BlockSpec((D_MODEL, D_MODEL), lambda j: (0, j)),
            pl.BlockSpec((1, D_MODEL), lambda j: (0, j)),
        ],
        out_specs=pl.BlockSpec((MOD_ROWS, D_MODEL), lambda j: (0, j)),
        out_shape=jax.ShapeDtypeStruct((MOD_ROWS, n), F32),
        name="mod",
    )(c_rows, w_mod, b_mod.reshape(1, n))


def _swiglu(h, wgu_ref, wd_ref):
    acc = None
    for c in range(D_FF // FF_CHUNK):
        lo = c * FF_CHUNK
        g = _dot(h, wgu_ref[:, lo:lo + FF_CHUNK])
        u = _dot(h, wgu_ref[:, D_FF + lo:D_FF + lo + FF_CHUNK])
        a = (g * _sigmoid(g) * u).astype(BF16)
        d = _dot(a, wd_ref[lo:lo + FF_CHUNK, :])
        acc = d if acc is None else acc + d
    return acc


def _ffn_sublayer(x, mod_ref, npre_ref, npost_ref, idx, wgu_ref, wd_ref):
    shift = mod_ref[0, 3 * idx:3 * idx + 1, :]
    scale = mod_ref[0, 3 * idx + 1:3 * idx + 2, :]
    gate = mod_ref[0, 3 * idx + 2:3 * idx + 3, :]
    h = _rms(x, npre_ref[idx:idx + 1, :]) * (1.0 + scale) + shift
    y = _swiglu(h.astype(BF16), wgu_ref, wd_ref)
    return x + 0.5 * gate * _rms(y, npost_ref[idx:idx + 1, :])


def _layer_in_kernel(x_ref, mod_ref, npre_ref, npost_ref, wgu_ref, wd_ref, win_ref, *out_refs,
                     proj_cols, write_x):
    x1 = _ffn_sublayer(x_ref[0], mod_ref, npre_ref, npost_ref, 0, wgu_ref, wd_ref)
    outs = list(out_refs)
    if write_x:
        outs.pop(0)[0] = x1
    shift = mod_ref[0, 3:4, :]
    scale = mod_ref[0, 4:5, :]
    h = (_rms(x1, npre_ref[1:2, :]) * (1.0 + scale) + shift).astype(BF16)
    for o_ref, (lo, width, mult) in zip(outs, proj_cols):
        p = _dot(h, win_ref[:, lo:lo + width])
        if mult != 1.0:
            p = p * mult
        o_ref[0] = p.astype(o_ref.dtype)


def _layer_in(x, mod, npre, npost, wgu, wd, win, proj, write_x):
    b, t, d = x.shape
    tm = min(TOKEN_TILE, t)
    tok = lambda w: pl.BlockSpec((1, tm, w), lambda i, j: (i, j, 0))
    out_shape, out_specs = [], []
    if write_x:
        out_shape.append(jax.ShapeDtypeStruct((b, t, d), F32))
        out_specs.append(tok(d))
    for _, width, _, dtype in proj:
        out_shape.append(jax.ShapeDtypeStruct((b, t, width), dtype))
        out_specs.append(tok(width))
    return pl.pallas_call(
        functools.partial(_layer_in_kernel, proj_cols=tuple(p[:3] for p in proj), write_x=write_x),
        grid=(b, t // tm),
        in_specs=[
            tok(d),
            pl.BlockSpec((1, N_MOD, d), lambda i, j: (i, 0, 0)),
            _resident(npre.shape), _resident(npost.shape),
            _resident(wgu.shape), _resident(wd.shape), _resident(win.shape),
        ],
        out_specs=out_specs,
        out_shape=out_shape,
        compiler_params=pltpu.CompilerParams(
            dimension_semantics=("parallel", "parallel"), vmem_limit_bytes=VMEM_LIMIT_BYTES),
        name="layer_in",
    )(x, mod, npre, npost, wgu, wd, win)


def _layer_out_kernel(x_ref, yr_ref, ya_ref, mod_ref, npre_ref, npost_ref, wout_ref, wgu_ref, wd_ref,
                      o_ref):
    y = _dot(yr_ref[0], wout_ref[:D_REC, :]) + _dot(ya_ref[0], wout_ref[D_REC:, :])
    x2 = x_ref[0] + mod_ref[0, 5:6, :] * _rms(y, npost_ref[1:2, :])
    o_ref[0] = _ffn_sublayer(x2, mod_ref, npre_ref, npost_ref, 2, wgu_ref, wd_ref)


def _layer_out(x1, y_rec, y_att, mod, npre, npost, wout, wgu, wd):
    b, t, d = x1.shape
    tm = min(TOKEN_TILE, t)
    tok = lambda w: pl.BlockSpec((1, tm, w), lambda i, j: (i, j, 0))
    return pl.pallas_call(
        _layer_out_kernel,
        grid=(b, t // tm),
        in_specs=[
            tok(d), tok(D_REC), tok(D_ATT),
            pl.BlockSpec((1, N_MOD, d), lambda i, j: (i, 0, 0)),
            _resident(npre.shape), _resident(npost.shape),
            _resident(wout.shape), _resident(wgu.shape), _resident(wd.shape),
        ],
        out_specs=tok(d),
        out_shape=jax.ShapeDtypeStruct((b, t, d), F32),
        compiler_params=pltpu.CompilerParams(
            dimension_semantics=("parallel", "parallel"), vmem_limit_bytes=VMEM_LIMIT_BYTES),
        name="layer_out",
    )(x1, y_rec, y_att, mod, npre, npost, wout, wgu, wd)


def _scan_tile(a, u, carry, row, reverse):
    for s in (1, 2, 4):
        if reverse:
            keep = row < SUBLANES - s
            shift = SUBLANES - s
        else:
            keep = row >= s
            shift = s
        a_sh = jnp.where(keep, pltpu.roll(a, shift, 0), 1.0)
        u_sh = jnp.where(keep, pltpu.roll(u, shift, 0), 0.0)
        u = a * u_sh + u
        a = a * a_sh
    h = a * carry + u
    return h, (h[0:1, :] if reverse else h[SUBLANES - 1:SUBLANES, :])


def _rglru_kernel(xr_ref, xc_ref, gr_ref, cw_ref, cb_ref, wg_ref, bg_ref, lam_ref, o_ref,
                  pad_s, conv_s, af_s, uf_s, ab_s, ub_s):
    cg = REC_GROUP
    lam = lam_ref[0]
    log_sig_lam = jnp.minimum(lam, 0.0) - jnp.log(1.0 + jnp.exp(-jnp.abs(lam)))
    row = lax.broadcasted_iota(jnp.int32, (SUBLANES, cg), 0)

    def conv_gates(src_ref, n):
        halo = SUBLANES
        pad_s[0:halo, :] = jnp.zeros((halo, cg), F32)
        pad_s[halo:halo + n, :] = src_ref[0]
        pad_s[halo + n:2 * halo + n, :] = jnp.zeros((halo, cg), F32)
        y = cb_ref[...]
        for k in range(CONV_W):
            off = halo + k - CONV_PAD_LEFT
            y = y + pad_s[off:off + n, :] * cw_ref[k:k + 1, :]
        conv_s[0:n, :] = y

        chunk = min(GATE_CHUNK, n)

        def gate_body(ci, _):
            r0 = pl.multiple_of(ci * chunk, chunk)
            xv = conv_s[pl.ds(r0, chunk), :]
            g = _dot(xv.astype(BF16), wg_ref[0]) + bg_ref[0]
            for d, (a_s, u_s) in enumerate(((af_s, uf_s), (ab_s, ub_s))):
                ga = g[:, (2 * d) * cg:(2 * d + 1) * cg]
                gx = g[:, (2 * d + 1) * cg:(2 * d + 2) * cg]
                log_a = LRU_C * _sigmoid(ga) * log_sig_lam[:, d * cg:(d + 1) * cg]
                a = jnp.exp(log_a)
                a_s[pl.ds(r0, chunk), :] = a
                one_minus_a2 = -jnp.tanh(log_a) * (a * a + 1.0)
                u_s[pl.ds(r0, chunk), :] = jnp.sqrt(one_minus_a2) * _sigmoid(gx) * xv
            return 0

        lax.fori_loop(0, n // chunk, gate_body, 0)

    def scan(n, hf0, hb0):
        tiles = n // SUBLANES

        def body(k, carry):
            hf, hb = carry
            rf = pl.multiple_of(k * SUBLANES, SUBLANES)
            rb = pl.multiple_of((tiles - 1 - k) * SUBLANES, SUBLANES)
            h, hf = _scan_tile(af_s[pl.ds(rf, SUBLANES), :], uf_s[pl.ds(rf, SUBLANES), :], hf, row, False)
            uf_s[pl.ds(rf, SUBLANES), :] = h
            h, hb = _scan_tile(ab_s[pl.ds(rb, SUBLANES), :], ub_s[pl.ds(rb, SUBLANES), :], hb, row, True)
            ub_s[pl.ds(rb, SUBLANES), :] = h
            return hf, hb

        return lax.fori_loop(0, tiles, body, (hf0, hb0), unroll=SCAN_UNROLL)

    n_ctx = xc_ref.shape[1]
    n_lat = xr_ref.shape[1]
    zero = jnp.zeros((1, cg), F32)
    conv_gates(xc_ref, n_ctx)
    hf, hb = scan(n_ctx, zero, zero)
    conv_gates(xr_ref, n_lat)
    scan(n_lat, hf, hb)
    y = uf_s[0:n_lat, :] + ub_s[0:n_lat, :]
    o_ref[0] = (y * jax.nn.gelu(gr_ref[0])).astype(o_ref.dtype)


def _rglru(xr, xr_ctx, gr, conv_w, conv_b, wg, bg, lam):
    b, t, _ = xr.shape
    n_ctx = xr_ctx.shape[1]
    cg = REC_GROUP
    seq = lambda n: pl.BlockSpec((1, n, cg), lambda i, j: (i, 0, j))
    grp = lambda shape: pl.BlockSpec((1,) + shape, lambda i, j: (j, 0, 0))
    return pl.pallas_call(
        _rglru_kernel,
        grid=(b, D_REC // cg),
        in_specs=[
            seq(t), seq(n_ctx), seq(t),
            pl.BlockSpec((CONV_W, cg), lambda i, j: (0, j)),
            pl.BlockSpec((1, cg), lambda i, j: (0, j)),
            grp((cg, 4 * cg)), grp((1, 4 * cg)), grp((1, 2 * cg)),
        ],
        out_specs=seq(t),
        out_shape=jax.ShapeDtypeStruct((b, t, D_REC), BF16),
        scratch_shapes=[pltpu.VMEM((t + 2 * SUBLANES, cg), F32)] + [pltpu.VMEM((t, cg), F32)] * 5,
        compiler_params=pltpu.CompilerParams(
            dimension_semantics=("parallel", "parallel"), vmem_limit_bytes=VMEM_LIMIT_BYTES),
        name="rglru",
    )(xr, xr_ctx, gr, conv_w, conv_b.reshape(1, D_REC), wg, bg, lam)


def _rglru_params(lru_wa, lru_ba, lru_wx, lru_bx, lru_lambda):
    cg = REC_GROUP
    n_grp = D_REC // cg
    per = cg // REC_BLOCK

    def block_diag(w):
        w = w.reshape(n_grp, per, REC_BLOCK, REC_BLOCK)
        eye = jnp.eye(per, dtype=w.dtype)
        return jnp.einsum("gpcd,pq->gpcqd", w, eye).reshape(n_grp, cg, cg)

    wg = jnp.concatenate([block_diag(lru_wa[0]), block_diag(lru_wx[0]),
                          block_diag(lru_wa[1]), block_diag(lru_wx[1])], axis=-1).astype(BF16)
    vec = lambda v: v.reshape(n_grp, 1, cg)
    bg = jnp.concatenate([vec(lru_ba[0]), vec(lru_bx[0]), vec(lru_ba[1]), vec(lru_bx[1])], axis=-1)
    lam = jnp.concatenate([vec(lru_lambda[0]), vec(lru_lambda[1])], axis=-1)
    return wg, bg, lam


def _attn_kernel(q_ref, k_ref, v_ref, kc_ref, vc_ref, tbl_ref, o_ref, *, rows):
    i = pl.program_id(2)
    n_blk = pl.num_programs(2)
    start = jnp.clip(ATT_Q_ROWS * i - WIN_ROWS // 2, 0, rows - ATT_BAND_ROWS)
    kind = jnp.where(i == 0, 0, jnp.where(i == n_blk - 1, 2, 1))
    off = pl.multiple_of(start * GRID_W, GRID_W)
    band = ATT_BAND_ROWS * GRID_W
    kb = k_ref[0, pl.ds(off, band), :]
    vb = v_ref[0, pl.ds(off, band), :]
    kc = kc_ref[0]
    vc = vc_ref[0]
    q = q_ref[0]
    lane = lax.broadcasted_iota(jnp.int32, q.shape, 1)
    contract_last = (((1,), (1,)), ((), ()))
    outs = []
    for hh in range(HEAD_PAIR):
        mine = (lane >= hh * HEAD_DIM) & (lane < (hh + 1) * HEAD_DIM)
        qm = jnp.where(mine, q, jnp.zeros_like(q))
        s_loc = lax.dot_general(qm, kb, contract_last, preferred_element_type=F32) + tbl_ref[kind, hh]
        s_ctx = lax.dot_general(qm, kc, contract_last, preferred_element_type=F32)
        m = jnp.maximum(jnp.max(s_loc, axis=-1, keepdims=True), jnp.max(s_ctx, axis=-1, keepdims=True))
        p_loc = jnp.exp(s_loc - m)
        p_ctx = jnp.exp(s_ctx - m)
        denom = jnp.sum(p_loc, axis=-1, keepdims=True) + jnp.sum(p_ctx, axis=-1, keepdims=True)
        o = _dot(p_loc.astype(BF16), vb) + _dot(p_ctx.astype(BF16), vc)
        outs.append(o / denom)
    o_ref[0] = jnp.where(lane < HEAD_DIM, outs[0], outs[1]).astype(o_ref.dtype)


def _attention(q, k, v, k_ctx, v_ctx, tbl):
    b, t, _ = q.shape
    n_ctx = k_ctx.shape[1]
    rows = t // GRID_W
    tq = ATT_Q_ROWS * GRID_W
    band = ATT_BAND_ROWS * GRID_W
    full = lambda n: pl.BlockSpec((1, n, LANES), lambda bi, p, i: (bi, 0, p))
    blk = pl.BlockSpec((1, tq, LANES), lambda bi, p, i: (bi, i, p))
    return pl.pallas_call(
        functools.partial(_attn_kernel, rows=rows),
        grid=(b, D_ATT // LANES, rows // ATT_Q_ROWS),
        in_specs=[
            blk, full(t), full(t), full(n_ctx), full(n_ctx),
            pl.BlockSpec((3, HEAD_PAIR, tq, band), lambda bi, p, i: (0, p, 0, 0)),
        ],
        out_specs=blk,
        out_shape=jax.ShapeDtypeStruct((b, t, D_ATT), BF16),
        compiler_params=pltpu.CompilerParams(
            dimension_semantics=("parallel", "parallel", "arbitrary"),
            vmem_limit_bytes=VMEM_LIMIT_BYTES),
        name="attn",
    )(q, k, v, k_ctx, v_ctx, tbl)


def _bias_tables(rpb, rows):
    n_blk = rows // ATT_Q_ROWS
    n_off = 2 * WIN_ROWS - 1
    col = np.arange(GRID_W)
    col_start = np.clip(col - WIN_COLS // 2, 0, GRID_W - WIN_COLS)
    col_valid = (col[None, :] >= col_start[:, None]) & (col[None, :] < col_start[:, None] + WIN_COLS)
    col_off = np.clip(col[None, :] - col[:, None] + WIN_COLS - 1, 0, 2 * WIN_COLS - 2)
    onehot = (col_off[:, :, None] == np.arange(2 * WIN_COLS - 1)).astype(np.float32)
    by_off = jnp.einsum("hrj,qkj->hrqk", rpb, onehot, precision=lax.Precision.HIGHEST)
    by_off = jnp.pad(by_off, ((0, 0), (ATT_BAND_ROWS, ATT_BAND_ROWS), (0, 0), (0, 0)))
    tables = []
    for i in (0, 1, n_blk - 1):
        band_start = int(np.clip(ATT_Q_ROWS * i - WIN_ROWS // 2, 0, rows - ATT_BAND_ROWS))
        slabs, valid = [], []
        for rq in range(ATT_Q_ROWS):
            r = ATT_Q_ROWS * i + rq
            kr = band_start + np.arange(ATT_BAND_ROWS)
            win_start = int(np.clip(r - WIN_ROWS // 2, 0, rows - WIN_ROWS))
            row_valid = (kr >= win_start) & (kr < win_start + WIN_ROWS)
            first_off = band_start - r + WIN_ROWS - 1
            assert -ATT_BAND_ROWS <= first_off <= n_off
            lo = first_off + ATT_BAND_ROWS
            slabs.append(by_off[:, lo:lo + ATT_BAND_ROWS])
            valid.append(row_valid[None, :, None] & col_valid[:, None, :])
        bias = jnp.stack(slabs, axis=1).transpose(0, 1, 3, 2, 4)
        bias = bias.reshape(N_ATT_HEADS, ATT_Q_ROWS * GRID_W, ATT_BAND_ROWS * GRID_W)
        mask = np.stack(valid).reshape(ATT_Q_ROWS * GRID_W, ATT_BAND_ROWS * GRID_W)
        tables.append(jnp.where(mask[None], bias, NEG_INF))
    return jnp.stack(tables)


def kernel(x, c, ctx, c_ctx, w_mod, b_mod, norm_pre, norm_post, ffn1_w_gu, ffn1_w_down, ffn2_w_gu,
           ffn2_w_down, w_in, w_out, conv_w, conv_b, lru_wa, lru_ba, lru_wx, lru_bx, lru_lambda, na_rpb):
    assert w_mod.shape[0] == 1, "single-layer problem"
    b, t, d = x.shape
    rows = t // GRID_W
    assert d == D_MODEL and t % (ATT_Q_ROWS * GRID_W) == 0 and rows >= 3 * ATT_Q_ROWS
    assert b + 1 <= MOD_ROWS

    c_rows = jnp.concatenate([c, c_ctx[None], jnp.zeros((MOD_ROWS - b - 1, d), F32)], axis=0)
    mod_all = _modulation(c_rows, w_mod[0], b_mod[0])
    mod = mod_all[:b].reshape(b, N_MOD, d)
    mod_ctx = jnp.broadcast_to(mod_all[b].reshape(1, N_MOD, d), (b, N_MOD, d))

    npre, npost = norm_pre[0], norm_post[0]
    wgu1, wd1 = ffn1_w_gu[0].astype(BF16), ffn1_w_down[0].astype(BF16)
    wgu2, wd2 = ffn2_w_gu[0].astype(BF16), ffn2_w_down[0].astype(BF16)
    win, wout = w_in[0].astype(BF16), w_out[0].astype(BF16)

    q_scale = HEAD_DIM ** -0.5
    x1, xr, gr, q, k, v = _layer_in(
        x, mod, npre, npost, wgu1, wd1, win,
        proj=((0, D_REC, 1.0, F32), (C_RG, D_REC, 1.0, F32), (C_Q, D_ATT, q_scale, BF16),
              (C_K, D_ATT, 1.0, BF16), (C_V, D_ATT, 1.0, BF16)),
        write_x=True)
    xr_ctx, k_ctx, v_ctx = _layer_in(
        ctx, mod_ctx, npre, npost, wgu1, wd1, win,
        proj=((0, D_REC, 1.0, F32), (C_K, D_ATT, 1.0, BF16), (C_V, D_ATT, 1.0, BF16)),
        write_x=False)

    wg, bg, lam = _rglru_params(lru_wa[0], lru_ba[0], lru_wx[0], lru_bx[0], lru_lambda[0])
    y_rec = _rglru(xr, xr_ctx, gr, conv_w[0], conv_b[0], wg, bg, lam)
    y_att = _attention(q, k, v, k_ctx, v_ctx, _bias_tables(na_rpb[0].astype(F32), rows))
    return _layer_out(x1, y_rec, y_att, mod, npre, npost, wout, wgu2, wd2)
```

```python
import functools

import numpy as np
import jax
import jax.numpy as jnp
from jax import lax
from jax.experimental import pallas as pl
from jax.experimental.pallas import tpu as pltpu

F32 = jnp.float32
BF16 = jnp.bfloat16

D_MODEL = 1024
D_REC = 512
D_ATT = 512
HEAD_DIM = 64
N_ATT_HEADS = 8
N_REC_BLOCKS = 8
REC_BLOCK = 64
CONV_W = 4
CONV_PAD_LEFT = 2
LRU_C = 8.0
GRID_W = 64
WIN_ROWS = 8
WIN_COLS = 16
D_FF = 2816
N_MOD = 9
EPS = 1e-6
NEG_INF = -1e30
C_RG, C_Q, C_K, C_V = 512, 1024, 1536, 2048
D_IN = 2560

LANES = 128
SUBLANES = 8
VMEM_LIMIT_BYTES = 56 * 1024 * 1024

TOKEN_TILE = 512
FF_CHUNK = 256
MOD_ROWS = 16
REC_GROUP = LANES
GATE_CHUNK = 512
SCAN_UNROLL = 16
ATT_Q_ROWS = 4
ATT_BAND_ROWS = 12
HEAD_PAIR = LANES // HEAD_DIM


def _sigmoid(x):
    return 0.5 * (jnp.tanh(0.5 * x) + 1.0)


def _rms(x, g):
    return x * lax.rsqrt(jnp.mean(x * x, axis=-1, keepdims=True) + EPS) * g


def _dot(a, b):
    return jnp.dot(a, b, preferred_element_type=F32)


def _resident(shape):
    return pl.BlockSpec(shape, lambda *_: (0,) * len(shape), pipeline_mode=pl.Buffered(1))


def _mod_kernel(c_ref, w_ref, b_ref, o_ref):
    c = c_ref[...]
    s = (c * _sigmoid(c)).astype(BF16)
    o_ref[...] = _dot(s, w_ref[...].astype(BF16)) + b_ref[...]


def _modulation(c_rows, w_mod, b_mod):
    n = w_mod.shape[1]
    return pl.pallas_call(
        _mod_kernel,
        grid=(n // D_MODEL,),
        in_specs=[
            pl.BlockSpec((MOD_ROWS, D_MODEL), lambda j: (0, 0)),
            pl.BlockSpec((D_MODEL, D_MODEL), lambda j: (0, j)),
            pl.BlockSpec((1, D_MODEL), lambda j: (0, j)),
        ],
        out_specs=pl.BlockSpec((MOD_ROWS, D_MODEL), lambda j: (0, j)),
        out_shape=jax.ShapeDtypeStruct((MOD_ROWS, n), F32),
        name="mod",
    )(c_rows, w_mod, b_mod.reshape(1, n))


def _swiglu(h, wgu_ref, wd_ref):
    acc = None
    for c in range(D_FF // FF_CHUNK):
        lo = c * FF_CHUNK
        g = _dot(h, wgu_ref[:, lo:lo + FF_CHUNK])
        u = _dot(h, wgu_ref[:, D_FF + lo:D_FF + lo + FF_CHUNK])
        a = (g * _sigmoid(g) * u).astype(BF16)
        d = _dot(a, wd_ref[lo:lo + FF_CHUNK, :])
        acc = d if acc is None else acc + d
    return acc


def _ffn_sublayer(x, mod_ref, npre_ref, npost_ref, idx, wgu_ref, wd_ref):
    shift = mod_ref[0, 3 * idx:3 * idx + 1, :]
    scale = mod_ref[0, 3 * idx + 1:3 * idx + 2, :]
    gate = mod_ref[0, 3 * idx + 2:3 * idx + 3, :]
    h = _rms(x, npre_ref[idx:idx + 1, :]) * (1.0 + scale) + shift
    y = _swiglu(h.astype(BF16), wgu_ref, wd_ref)
    return x + 0.5 * gate * _rms(y, npost_ref[idx:idx + 1, :])


def _layer_in_kernel(x_ref, mod_ref, npre_ref, npost_ref, wgu_ref, wd_ref, win_ref, *out_refs,
                     proj_cols, write_x):
    x1 = _ffn_sublayer(x_ref[0], mod_ref, npre_ref, npost_ref, 0, wgu_ref, wd_ref)
    outs = list(out_refs)
    if write_x:
        outs.pop(0)[0] = x1
    shift = mod_ref[0, 3:4, :]
    scale = mod_ref[0, 4:5, :]
    h = (_rms(x1, npre_ref[1:2, :]) * (1.0 + scale) + shift).astype(BF16)
    for o_ref, (lo, width, mult) in zip(outs, proj_cols):
        p = _dot(h, win_ref[:, lo:lo + width])
        if mult != 1.0:
            p = p * mult
        o_ref[0] = p.astype(o_ref.dtype)


def _layer_in(x, mod, npre, npost, wgu, wd, win, proj, write_x):
    b, t, d = x.shape
    tm = min(TOKEN_TILE, t)
    tok = lambda w: pl.BlockSpec((1, tm, w), lambda i, j: (i, j, 0))
    out_shape, out_specs = [], []
    if write_x:
        out_shape.append(jax.ShapeDtypeStruct((b, t, d), F32))
        out_specs.append(tok(d))
    for _, width, _, dtype in proj:
        out_shape.append(jax.ShapeDtypeStruct((b, t, width), dtype))
        out_specs.append(tok(width))
    return pl.pallas_call(
        functools.partial(_layer_in_kernel, proj_cols=tuple(p[:3] for p in proj), write_x=write_x),
        grid=(b, t // tm),
        in_specs=[
            tok(d),
            pl.BlockSpec((1, N_MOD, d), lambda i, j: (i, 0, 0)),
            _resident(npre.shape), _resident(npost.shape),
            _resident(wgu.shape), _resident(wd.shape), _resident(win.shape),
        ],
        out_specs=out_specs,
        out_shape=out_shape,
        compiler_params=pltpu.CompilerParams(
            dimension_semantics=("parallel", "parallel"), vmem_limit_bytes=VMEM_LIMIT_BYTES),
        name="layer_in",
    )(x, mod, npre, npost, wgu, wd, win)


def _layer_out_kernel(x_ref, yr_ref, ya_ref, mod_ref, npre_ref, npost_ref, wout_ref, wgu_ref, wd_ref,
                      o_ref):
    y = _dot(yr_ref[0], wout_ref[:D_REC, :]) + _dot(ya_ref[0], wout_ref[D_REC:, :])
    x2 = x_ref[0] + mod_ref[0, 5:6, :] * _rms(y, npost_ref[1:2, :])
    o_ref[0] = _ffn_sublayer(x2, mod_ref, npre_ref, npost_ref, 2, wgu_ref, wd_ref)


def _layer_out(x1, y_rec, y_att, mod, npre, npost, wout, wgu, wd):
    b, t, d = x1.shape
    tm = min(TOKEN_TILE, t)
    tok = lambda w: pl.BlockSpec((1, tm, w), lambda i, j: (i, j, 0))
    return pl.pallas_call(
        _layer_out_kernel,
        grid=(b, t // tm),
        in_specs=[
            tok(d), tok(D_REC), tok(D_ATT),
            pl.BlockSpec((1, N_MOD, d), lambda i, j: (i, 0, 0)),
            _resident(npre.shape), _resident(npost.shape),
            _resident(wout.shape), _resident(wgu.shape), _resident(wd.shape),
        ],
        out_specs=tok(d),
        out_shape=jax.ShapeDtypeStruct((b, t, d), F32),
        compiler_params=pltpu.CompilerParams(
            dimension_semantics=("parallel", "parallel"), vmem_limit_bytes=VMEM_LIMIT_BYTES),
        name="layer_out",
    )(x1, y_rec, y_att, mod, npre, npost, wout, wgu, wd)


def _scan_tile(a, u, carry, row, reverse):
    for s in (1, 2, 4):
        if reverse:
            keep = row < SUBLANES - s
            shift = SUBLANES - s
        else:
            keep = row >= s
            shift = s
        a_sh = jnp.where(keep, pltpu.roll(a, shift, 0), 1.0)
        u_sh = jnp.where(keep, pltpu.roll(u, shift, 0), 0.0)
        u = a * u_sh + u
        a = a * a_sh
    last = 0 if reverse else SUBLANES - 1
    return a * carry + u, a[last:last + 1, :] * carry + u[last:last + 1, :]


def _rglru_kernel(xr_ref, xc_ref, gr_ref, cw_ref, cb_ref, wg_ref, bg_ref, lam_ref, o_ref,
                  pad_s, conv_s, af_s, uf_s, ab_s, ub_s, hf_s, hb_s):
    cg = REC_GROUP
    lam = lam_ref[0]
    log_sig_lam = jnp.minimum(lam, 0.0) - jnp.log(1.0 + jnp.exp(-jnp.abs(lam)))
    row = lax.broadcasted_iota(jnp.int32, (SUBLANES, cg), 0)

    def conv_gates(src_ref, n):
        halo = SUBLANES
        pad_s[0:halo, :] = jnp.zeros((halo, cg), F32)
        pad_s[halo:halo + n, :] = src_ref[0]
        pad_s[halo + n:2 * halo + n, :] = jnp.zeros((halo, cg), F32)
        y = cb_ref[...]
        for k in range(CONV_W):
            off = halo + k - CONV_PAD_LEFT
            y = y + pad_s[off:off + n, :] * cw_ref[k:k + 1, :]
        conv_s[0:n, :] = y

        chunk = min(GATE_CHUNK, n)

        def gate_body(ci, _):
            r0 = pl.multiple_of(ci * chunk, chunk)
            xv = conv_s[pl.ds(r0, chunk), :]
            g = _dot(xv.astype(BF16), wg_ref[0]) + bg_ref[0]
            for d, (a_s, u_s) in enumerate(((af_s, uf_s), (ab_s, ub_s))):
                ga = g[:, (2 * d) * cg:(2 * d + 1) * cg]
                gx = g[:, (2 * d + 1) * cg:(2 * d + 2) * cg]
                log_a = LRU_C * _sigmoid(ga) * log_sig_lam[:, d * cg:(d + 1) * cg]
                a = jnp.exp(log_a)
                a_s[pl.ds(r0, chunk), :] = a
                one_minus_a2 = -jnp.tanh(log_a) * (a * a + 1.0)
                u_s[pl.ds(r0, chunk), :] = jnp.sqrt(one_minus_a2) * _sigmoid(gx) * xv
            return 0

        lax.fori_loop(0, n // chunk, gate_body, 0)

    def scan(n, hf0, hb0):
        tiles = n // SUBLANES

        def body(k, carry):
            hf, hb = carry
            rf = pl.multiple_of(k * SUBLANES, SUBLANES)
            rb = pl.multiple_of((tiles - 1 - k) * SUBLANES, SUBLANES)
            h, hf = _scan_tile(af_s[pl.ds(rf, SUBLANES), :], uf_s[pl.ds(rf, SUBLANES), :], hf, row, False)
            hf_s[pl.ds(rf, SUBLANES), :] = h
            h, hb = _scan_tile(ab_s[pl.ds(rb, SUBLANES), :], ub_s[pl.ds(rb, SUBLANES), :], hb, row, True)
            hb_s[pl.ds(rb, SUBLANES), :] = h
            return hf, hb

        return lax.fori_loop(0, tiles, body, (hf0, hb0), unroll=SCAN_UNROLL)

    n_ctx = xc_ref.shape[1]
    n_lat = xr_ref.shape[1]
    zero = jnp.zeros((1, cg), F32)
    conv_gates(xc_ref, n_ctx)
    hf, hb = scan(n_ctx, zero, zero)
    conv_gates(xr_ref, n_lat)
    scan(n_lat, hf, hb)
    y = hf_s[0:n_lat, :] + hb_s[0:n_lat, :]
    o_ref[0] = (y * jax.nn.gelu(gr_ref[0])).astype(o_ref.dtype)


def _rglru(xr, xr_ctx, gr, conv_w, conv_b, wg, bg, lam):
    b, t, _ = xr.shape
    n_ctx = xr_ctx.shape[1]
    cg = REC_GROUP
    seq = lambda n: pl.BlockSpec((1, n, cg), lambda i, j: (i, 0, j))
    grp = lambda shape: pl.BlockSpec((1,) + shape, lambda i, j: (j, 0, 0))
    return pl.pallas_call(
        _rglru_kernel,
        grid=(b, D_REC // cg),
        in_specs=[
            seq(t), seq(n_ctx), seq(t),
            pl.BlockSpec((CONV_W, cg), lambda i, j: (0, j)),
            pl.BlockSpec((1, cg), lambda i, j: (0, j)),
            grp((cg, 4 * cg)), grp((1, 4 * cg)), grp((1, 2 * cg)),
        ],
        out_specs=seq(t),
        out_shape=jax.ShapeDtypeStruct((b, t, D_REC), BF16),
        scratch_shapes=[pltpu.VMEM((t + 2 * SUBLANES, cg), F32)] + [pltpu.VMEM((t, cg), F32)] * 7,
        compiler_params=pltpu.CompilerParams(
            dimension_semantics=("parallel", "parallel"), vmem_limit_bytes=VMEM_LIMIT_BYTES),
        name="rglru",
    )(xr, xr_ctx, gr, conv_w, conv_b.reshape(1, D_REC), wg, bg, lam)


def _rglru_params(lru_wa, lru_ba, lru_wx, lru_bx, lru_lambda):
    cg = REC_GROUP
    n_grp = D_REC // cg
    per = cg // REC_BLOCK

    def block_diag(w):
        w = w.reshape(n_grp, per, REC_BLOCK, REC_BLOCK)
        eye = jnp.eye(per, dtype=w.dtype)
        return jnp.einsum("gpcd,pq->gpcqd", w, eye).reshape(n_grp, cg, cg)

    wg = jnp.concatenate([block_diag(lru_wa[0]), block_diag(lru_wx[0]),
                          block_diag(lru_wa[1]), block_diag(lru_wx[1])], axis=-1).astype(BF16)
    vec = lambda v: v.reshape(n_grp, 1, cg)
    bg = jnp.concatenate([vec(lru_ba[0]), vec(lru_bx[0]), vec(lru_ba[1]), vec(lru_bx[1])], axis=-1)
    lam = jnp.concatenate([vec(lru_lambda[0]), vec(lru_lambda[1])], axis=-1)
    return wg, bg, lam


def _attn_kernel(q_ref, k_ref, v_ref, kc_ref, vc_ref, tbl_ref, o_ref, *, rows):
    i = pl.program_id(1)
    n_blk = pl.num_programs(1)
    start = jnp.clip(ATT_Q_ROWS * i - WIN_ROWS // 2, 0, rows - ATT_BAND_ROWS)
    kind = jnp.where(i == 0, 0, jnp.where(i == n_blk - 1, 2, 1))
    off = pl.multiple_of(start * GRID_W, GRID_W)
    band = ATT_BAND_ROWS * GRID_W
    lane = lax.broadcasted_iota(jnp.int32, (ATT_Q_ROWS * GRID_W, LANES), 1)
    contract_last = (((1,), (1,)), ((), ()))
    for pair in range(D_ATT // LANES):
        cols = slice(pair * LANES, (pair + 1) * LANES)
        kb = k_ref[0, pl.ds(off, band), cols]
        vb = v_ref[0, pl.ds(off, band), cols]
        kc = kc_ref[0, :, cols]
        vc = vc_ref[0, :, cols]
        q = q_ref[0, :, cols]
        outs = []
        for hh in range(HEAD_PAIR):
            mine = (lane >= hh * HEAD_DIM) & (lane < (hh + 1) * HEAD_DIM)
            qm = jnp.where(mine, q, jnp.zeros_like(q))
            s_loc = lax.dot_general(qm, kb, contract_last, preferred_element_type=F32)
            s_ctx = lax.dot_general(qm, kc, contract_last, preferred_element_type=F32)
            s = jnp.concatenate([s_loc + tbl_ref[kind, pair * HEAD_PAIR + hh], s_ctx], axis=-1)
            e = jnp.exp(s - jnp.max(s, axis=-1, keepdims=True))
            denom = jnp.sum(e, axis=-1, keepdims=True)
            eb = e.astype(BF16)
            outs.append((_dot(eb[:, :band], vb) + _dot(eb[:, band:], vc)) / denom)
        o_ref[0, :, cols] = jnp.where(lane < HEAD_DIM, outs[0], outs[1]).astype(o_ref.dtype)


def _attention(q, k, v, k_ctx, v_ctx, tbl):
    b, t, _ = q.shape
    n_ctx = k_ctx.shape[1]
    rows = t // GRID_W
    tq = ATT_Q_ROWS * GRID_W
    full = lambda n: pl.BlockSpec((1, n, D_ATT), lambda bi, i: (bi, 0, 0))
    blk = pl.BlockSpec((1, tq, D_ATT), lambda bi, i: (bi, i, 0))
    return pl.pallas_call(
        functools.partial(_attn_kernel, rows=rows),
        grid=(b, rows // ATT_Q_ROWS),
        in_specs=[blk, full(t), full(t), full(n_ctx), full(n_ctx), _resident(tbl.shape)],
        out_specs=blk,
        out_shape=jax.ShapeDtypeStruct((b, t, D_ATT), BF16),
        compiler_params=pltpu.CompilerParams(
            dimension_semantics=("parallel", "arbitrary"), vmem_limit_bytes=VMEM_LIMIT_BYTES),
        name="attn",
    )(q, k, v, k_ctx, v_ctx, tbl)


def _bias_tables(rpb, rows):
    n_blk = rows // ATT_Q_ROWS
    n_off = 2 * WIN_ROWS - 1
    col = np.arange(GRID_W)
    col_start = np.clip(col - WIN_COLS // 2, 0, GRID_W - WIN_COLS)
    col_valid = (col[None, :] >= col_start[:, None]) & (col[None, :] < col_start[:, None] + WIN_COLS)
    col_off = np.clip(col[None, :] - col[:, None] + WIN_COLS - 1, 0, 2 * WIN_COLS - 2)
    onehot = (col_off[:, :, None] == np.arange(2 * WIN_COLS - 1)).astype(np.float32)
    by_off = jnp.einsum("hrj,qkj->hrqk", rpb, onehot, precision=lax.Precision.HIGHEST)
    by_off = jnp.pad(by_off, ((0, 0), (ATT_BAND_ROWS, ATT_BAND_ROWS), (0, 0), (0, 0)))
    tables = []
    for i in (0, 1, n_blk - 1):
        band_start = int(np.clip(ATT_Q_ROWS * i - WIN_ROWS // 2, 0, rows - ATT_BAND_ROWS))
        slabs, valid = [], []
        for rq in range(ATT_Q_ROWS):
            r = ATT_Q_ROWS * i + rq
            kr = band_start + np.arange(ATT_BAND_ROWS)
            win_start = int(np.clip(r - WIN_ROWS // 2, 0, rows - WIN_ROWS))
            row_valid = (kr >= win_start) & (kr < win_start + WIN_ROWS)
            first_off = band_start - r + WIN_ROWS - 1
            assert -ATT_BAND_ROWS <= first_off <= n_off
            lo = first_off + ATT_BAND_ROWS
            slabs.append(by_off[:, lo:lo + ATT_BAND_ROWS])
            valid.append(row_valid[None, :, None] & col_valid[:, None, :])
        bias = jnp.stack(slabs, axis=1).transpose(0, 1, 3, 2, 4)
        bias = bias.reshape(N_ATT_HEADS, ATT_Q_ROWS * GRID_W, ATT_BAND_ROWS * GRID_W)
        mask = np.stack(valid).reshape(ATT_Q_ROWS * GRID_W, ATT_BAND_ROWS * GRID_W)
        tables.append(jnp.where(mask[None], bias, NEG_INF))
    return jnp.stack(tables)


def kernel(x, c, ctx, c_ctx, w_mod, b_mod, norm_pre, norm_post, ffn1_w_gu, ffn1_w_down, ffn2_w_gu,
           ffn2_w_down, w_in, w_out, conv_w, conv_b, lru_wa, lru_ba, lru_wx, lru_bx, lru_lambda, na_rpb):
    assert w_mod.shape[0] == 1, "single-layer problem"
    b, t, d = x.shape
    rows = t // GRID_W
    assert d == D_MODEL and t % (ATT_Q_ROWS * GRID_W) == 0 and rows >= 3 * ATT_Q_ROWS
    assert b + 1 <= MOD_ROWS

    c_rows = jnp.concatenate([c, c_ctx[None], jnp.zeros((MOD_ROWS - b - 1, d), F32)], axis=0)
    mod_all = _modulation(c_rows, w_mod[0], b_mod[0])
    mod = mod_all[:b].reshape(b, N_MOD, d)
    mod_ctx = jnp.broadcast_to(mod_all[b].reshape(1, N_MOD, d), (b, N_MOD, d))

    npre, npost = norm_pre[0], norm_post[0]
    wgu1, wd1 = ffn1_w_gu[0].astype(BF16), ffn1_w_down[0].astype(BF16)
    wgu2, wd2 = ffn2_w_gu[0].astype(BF16), ffn2_w_down[0].astype(BF16)
    win, wout = w_in[0].astype(BF16), w_out[0].astype(BF16)

    q_scale = HEAD_DIM ** -0.5
    x1, xr, gr, q, k, v = _layer_in(
        x, mod, npre, npost, wgu1, wd1, win,
        proj=((0, D_REC, 1.0, F32), (C_RG, D_REC, 1.0, F32), (C_Q, D_ATT, q_scale, BF16),
              (C_K, D_ATT, 1.0, BF16), (C_V, D_ATT, 1.0, BF16)),
        write_x=True)
    xr_ctx, k_ctx, v_ctx = _layer_in(
        ctx, mod_ctx, npre, npost, wgu1, wd1, win,
        proj=((0, D_REC, 1.0, F32), (C_K, D_ATT, 1.0, BF16), (C_V, D_ATT, 1.0, BF16)),
        write_x=False)

    wg, bg, lam = _rglru_params(lru_wa[0], lru_ba[0], lru_wx[0], lru_bx[0], lru_lambda[0])
    y_rec = _rglru(xr, xr_ctx, gr, conv_w[0], conv_b[0], wg, bg, lam)
    y_att = _attention(q, k, v, k_ctx, v_ctx, _bias_tables(na_rpb[0].astype(F32), rows))
    return _layer_out(x1, y_rec, y_att, mod, npre, npost, wout, wgu2, wd2)
```

```python
import functools

import numpy as np
import jax
import jax.numpy as jnp
from jax import lax
from jax.experimental import pallas as pl
from jax.experimental.pallas import tpu as pltpu

F32 = jnp.float32
BF16 = jnp.bfloat16

D_MODEL = 1024
D_REC = 512
D_ATT = 512
HEAD_DIM = 64
N_ATT_HEADS = 8
N_REC_BLOCKS = 8
REC_BLOCK = 64
CONV_W = 4
CONV_PAD_LEFT = 2
LRU_C = 8.0
GRID_W = 64
WIN_ROWS = 8
WIN_COLS = 16
D_FF = 2816
N_MOD = 9
EPS = 1e-6
NEG_INF = -1e30
C_RG, C_Q, C_K, C_V = 512, 1024, 1536, 2048
D_IN = 2560

LANES = 128
SUBLANES = 8
VMEM_LIMIT_BYTES = 56 * 1024 * 1024

TOKEN_TILE = 512
FF_CHUNK = 256
MOD_ROWS = 16
REC_GROUP = LANES
GATE_CHUNK = 512
SCAN_UNROLL = 16
ATT_Q_ROWS = 4
ATT_BAND_ROWS = 12
HEAD_PAIR = LANES // HEAD_DIM


def _sigmoid(x):
    return 0.5 * (jnp.tanh(0.5 * x) + 1.0)


def _rms(x, g):
    return x * lax.rsqrt(jnp.mean(x * x, axis=-1, keepdims=True) + EPS) * g


def _dot(a, b):
    return jnp.dot(a, b, preferred_element_type=F32)


def _resident(shape):
    return pl.BlockSpec(shape, lambda *_: (0,) * len(shape), pipeline_mode=pl.Buffered(1))


def _mod_kernel(c_ref, w_ref, b_ref, o_ref):
    c = c_ref[...]
    s = (c * _sigmoid(c)).astype(BF16)
    o_ref[...] = _dot(s, w_ref[...].astype(BF16)) + b_ref[...]


def _modulation(c_rows, w_mod, b_mod):
    n = w_mod.shape[1]
    return pl.pallas_call(
        _mod_kernel,
        grid=(n // D_MODEL,),
        in_specs=[
            pl.BlockSpec((MOD_ROWS, D_MODEL), lambda j: (0, 0)),
            pl.BlockSpec((D_MODEL, D_MODEL), lambda j: (0, j)),
            pl.BlockSpec((1, D_MODEL), lambda j: (0, j)),
        ],
        out_specs=pl.BlockSpec((MOD_ROWS, D_MODEL), lambda j: (0, j)),
        out_shape=jax.ShapeDtypeStruct((MOD_ROWS, n), F32),
        name="mod",
    )(c_rows, w_mod, b_mod.reshape(1, n))


def _swiglu(h, wgu_ref, wd_ref):
    acc = None
    for c in range(D_FF // FF_CHUNK):
        lo = c * FF_CHUNK
        g = _dot(h, wgu_ref[:, lo:lo + FF_CHUNK])
        u = _dot(h, wgu_ref[:, D_FF + lo:D_FF + lo + FF_CHUNK])
        a = (g * _sigmoid(g) * u).astype(BF16)
        d = _dot(a, wd_ref[lo:lo + FF_CHUNK, :])
        acc = d if acc is None else acc + d
    return acc


def _ffn_sublayer(x, mod_ref, npre_ref, npost_ref, idx, wgu_ref, wd_ref):
    shift = mod_ref[0, 3 * idx:3 * idx + 1, :]
    scale = mod_ref[0, 3 * idx + 1:3 * idx + 2, :]
    gate = mod_ref[0, 3 * idx + 2:3 * idx + 3, :]
    h = _rms(x, npre_ref[idx:idx + 1, :]) * (1.0 + scale) + shift
    y = _swiglu(h.astype(BF16), wgu_ref, wd_ref)
    return x + 0.5 * gate * _rms(y, npost_ref[idx:idx + 1, :])


def _layer_in_kernel(x_ref, mod_ref, npre_ref, npost_ref, wgu_ref, wd_ref, win_ref, *out_refs,
                     proj_cols, write_x):
    x1 = _ffn_sublayer(x_ref[0], mod_ref, npre_ref, npost_ref, 0, wgu_ref, wd_ref)
    outs = list(out_refs)
    if write_x:
        outs.pop(0)[0] = x1
    shift = mod_ref[0, 3:4, :]
    scale = mod_ref[0, 4:5, :]
    h = (_rms(x1, npre_ref[1:2, :]) * (1.0 + scale) + shift).astype(BF16)
    for o_ref, (lo, width, mult) in zip(outs, proj_cols):
        p = _dot(h, win_ref[:, lo:lo + width])
        if mult != 1.0:
            p = p * mult
        o_ref[0] = p.astype(o_ref.dtype)


def _layer_in(x, mod, npre, npost, wgu, wd, win, proj, write_x):
    b, t, d = x.shape
    tm = min(TOKEN_TILE, t)
    tok = lambda w: pl.BlockSpec((1, tm, w), lambda i, j: (i, j, 0))
    out_shape, out_specs = [], []
    if write_x:
        out_shape.append(jax.ShapeDtypeStruct((b, t, d), F32))
        out_specs.append(tok(d))
    for _, width, _, dtype in proj:
        out_shape.append(jax.ShapeDtypeStruct((b, t, width), dtype))
        out_specs.append(tok(width))
    return pl.pallas_call(
        functools.partial(_layer_in_kernel, proj_cols=tuple(p[:3] for p in proj), write_x=write_x),
        grid=(b, t // tm),
        in_specs=[
            tok(d),
            pl.BlockSpec((1, N_MOD, d), lambda i, j: (i, 0, 0)),
            _resident(npre.shape), _resident(npost.shape),
            _resident(wgu.shape), _resident(wd.shape), _resident(win.shape),
        ],
        out_specs=out_specs,
        out_shape=out_shape,
        compiler_params=pltpu.CompilerParams(
            dimension_semantics=("parallel", "parallel"), vmem_limit_bytes=VMEM_LIMIT_BYTES),
        name="layer_in",
    )(x, mod, npre, npost, wgu, wd, win)


def _layer_out_kernel(x_ref, yr_ref, ya_ref, mod_ref, npre_ref, npost_ref, wout_ref, wgu_ref, wd_ref,
                      o_ref):
    y = _dot(yr_ref[0], wout_ref[:D_REC, :]) + _dot(ya_ref[0], wout_ref[D_REC:, :])
    x2 = x_ref[0] + mod_ref[0, 5:6, :] * _rms(y, npost_ref[1:2, :])
    o_ref[0] = _ffn_sublayer(x2, mod_ref, npre_ref, npost_ref, 2, wgu_ref, wd_ref)


def _layer_out(x1, y_rec, y_att, mod, npre, npost, wout, wgu, wd):
    b, t, d = x1.shape
    tm = min(TOKEN_TILE, t)
    tok = lambda w: pl.BlockSpec((1, tm, w), lambda i, j: (i, j, 0))
    return pl.pallas_call(
        _layer_out_kernel,
        grid=(b, t // tm),
        in_specs=[
            tok(d), tok(D_REC), tok(D_ATT),
            pl.BlockSpec((1, N_MOD, d), lambda i, j: (i, 0, 0)),
            _resident(npre.shape), _resident(npost.shape),
            _resident(wout.shape), _resident(wgu.shape), _resident(wd.shape),
        ],
        out_specs=tok(d),
        out_shape=jax.ShapeDtypeStruct((b, t, d), F32),
        compiler_params=pltpu.CompilerParams(
            dimension_semantics=("parallel", "parallel"), vmem_limit_bytes=VMEM_LIMIT_BYTES),
        name="layer_out",
    )(x1, y_rec, y_att, mod, npre, npost, wout, wgu, wd)


def _scan_tile(a, u, carry, row, reverse):
    for s in (1, 2, 4):
        if reverse:
            keep = row < SUBLANES - s
            shift = SUBLANES - s
        else:
            keep = row >= s
            shift = s
        a_sh = jnp.where(keep, pltpu.roll(a, shift, 0), 1.0)
        u_sh = jnp.where(keep, pltpu.roll(u, shift, 0), 0.0)
        u = a * u_sh + u
        a = a * a_sh
    last = 0 if reverse else SUBLANES - 1
    return a * carry + u, a[last:last + 1, :] * carry + u[last:last + 1, :]


def _rglru_kernel(xr_ref, xc_ref, gr_ref, cw_ref, cb_ref, wg_ref, bg_ref, lam_ref, o_ref,
                  pad_s, conv_s, af_s, uf_s, ab_s, ub_s, hf_s, hb_s):
    cg = REC_GROUP
    lam = lam_ref[0]
    log_sig_lam = jnp.minimum(lam, 0.0) - jnp.log(1.0 + jnp.exp(-jnp.abs(lam)))
    neg_half_c_lsl = (-0.5 * LRU_C) * log_sig_lam
    row = lax.broadcasted_iota(jnp.int32, (SUBLANES, cg), 0)

    def conv_gates(src_ref, n):
        halo = SUBLANES
        pad_s[0:halo, :] = jnp.zeros((halo, cg), F32)
        pad_s[halo:halo + n, :] = src_ref[0]
        pad_s[halo + n:2 * halo + n, :] = jnp.zeros((halo, cg), F32)
        y = cb_ref[...]
        for k in range(CONV_W):
            off = halo + k - CONV_PAD_LEFT
            y = y + pad_s[off:off + n, :] * cw_ref[k:k + 1, :]
        conv_s[0:n, :] = y

        chunk = min(GATE_CHUNK, n)

        def gate_body(ci, _):
            r0 = pl.multiple_of(ci * chunk, chunk)
            xv = conv_s[pl.ds(r0, chunk), :]
            g = _dot(xv.astype(BF16), wg_ref[0]) + bg_ref[0]
            x_half = 0.5 * xv
            for d, (a_s, u_s) in enumerate(((af_s, uf_s), (ab_s, ub_s))):
                ta = jnp.tanh(g[:, (2 * d) * cg:(2 * d + 1) * cg])
                tx = jnp.tanh(g[:, (2 * d + 1) * cg:(2 * d + 2) * cg])
                neg_log_a = neg_half_c_lsl[:, d * cg:(d + 1) * cg] * (ta + 1.0)
                a = jnp.exp(-neg_log_a)
                a_s[pl.ds(r0, chunk), :] = a
                one_minus_a2 = jnp.tanh(neg_log_a) * (a * a + 1.0)
                root = jnp.where(one_minus_a2 > 0.0, one_minus_a2 * lax.rsqrt(one_minus_a2), 0.0)
                u_s[pl.ds(r0, chunk), :] = root * (tx + 1.0) * x_half
            return 0

        lax.fori_loop(0, n // chunk, gate_body, 0)

    def scan(n, hf0, hb0):
        tiles = n // SUBLANES

        def body(k, carry):
            hf, hb = carry
            rf = pl.multiple_of(k * SUBLANES, SUBLANES)
            rb = pl.multiple_of((tiles - 1 - k) * SUBLANES, SUBLANES)
            h, hf = _scan_tile(af_s[pl.ds(rf, SUBLANES), :], uf_s[pl.ds(rf, SUBLANES), :], hf, row, False)
            hf_s[pl.ds(rf, SUBLANES), :] = h
            h, hb = _scan_tile(ab_s[pl.ds(rb, SUBLANES), :], ub_s[pl.ds(rb, SUBLANES), :], hb, row, True)
            hb_s[pl.ds(rb, SUBLANES), :] = h
            return hf, hb

        return lax.fori_loop(0, tiles, body, (hf0, hb0), unroll=SCAN_UNROLL)

    n_ctx = xc_ref.shape[1]
    n_lat = xr_ref.shape[1]
    zero = jnp.zeros((1, cg), F32)
    conv_gates(xc_ref, n_ctx)
    hf, hb = scan(n_ctx, zero, zero)
    conv_gates(xr_ref, n_lat)
    scan(n_lat, hf, hb)
    y = hf_s[0:n_lat, :] + hb_s[0:n_lat, :]
    o_ref[0] = (y * jax.nn.gelu(gr_ref[0])).astype(o_ref.dtype)


def _rglru(xr, xr_ctx, gr, conv_w, conv_b, wg, bg, lam):
    b, t, _ = xr.shape
    n_ctx = xr_ctx.shape[1]
    cg = REC_GROUP
    seq = lambda n: pl.BlockSpec((1, n, cg), lambda i, j: (i, 0, j))
    grp = lambda shape: pl.BlockSpec((1,) + shape, lambda i, j: (j, 0, 0))
    return pl.pallas_call(
        _rglru_kernel,
        grid=(b, D_REC // cg),
        in_specs=[
            seq(t), seq(n_ctx), seq(t),
            pl.BlockSpec((CONV_W, cg), lambda i, j: (0, j)),
            pl.BlockSpec((1, cg), lambda i, j: (0, j)),
            grp((cg, 4 * cg)), grp((1, 4 * cg)), grp((1, 2 * cg)),
        ],
        out_specs=seq(t),
        out_shape=jax.ShapeDtypeStruct((b, t, D_REC), BF16),
        scratch_shapes=[pltpu.VMEM((t + 2 * SUBLANES, cg), F32)] + [pltpu.VMEM((t, cg), F32)] * 7,
        compiler_params=pltpu.CompilerParams(
            dimension_semantics=("parallel", "parallel"), vmem_limit_bytes=VMEM_LIMIT_BYTES),
        name="rglru",
    )(xr, xr_ctx, gr, conv_w, conv_b.reshape(1, D_REC), wg, bg, lam)


def _rglru_params(lru_wa, lru_ba, lru_wx, lru_bx, lru_lambda):
    cg = REC_GROUP
    n_grp = D_REC // cg
    per = cg // REC_BLOCK

    def block_diag(w):
        w = w.reshape(n_grp, per, REC_BLOCK, REC_BLOCK)
        eye = jnp.eye(per, dtype=w.dtype)
        return jnp.einsum("gpcd,pq->gpcqd", w, eye).reshape(n_grp, cg, cg)

    wg = (0.5 * jnp.concatenate([block_diag(lru_wa[0]), block_diag(lru_wx[0]),
                                 block_diag(lru_wa[1]), block_diag(lru_wx[1])], axis=-1)).astype(BF16)
    vec = lambda v: v.reshape(n_grp, 1, cg)
    bg = 0.5 * jnp.concatenate([vec(lru_ba[0]), vec(lru_bx[0]), vec(lru_ba[1]), vec(lru_bx[1])], axis=-1)
    lam = jnp.concatenate([vec(lru_lambda[0]), vec(lru_lambda[1])], axis=-1)
    return wg, bg, lam


def _attn_kernel(q_ref, k_ref, v_ref, kc_ref, vc_ref, tbl_ref, o_ref, *, rows):
    i = pl.program_id(1)
    n_blk = pl.num_programs(1)
    start = jnp.clip(ATT_Q_ROWS * i - WIN_ROWS // 2, 0, rows - ATT_BAND_ROWS)
    kind = jnp.where(i == 0, 0, jnp.where(i == n_blk - 1, 2, 1))
    off = pl.multiple_of(start * GRID_W, GRID_W)
    band = ATT_BAND_ROWS * GRID_W
    lane = lax.broadcasted_iota(jnp.int32, (ATT_Q_ROWS * GRID_W, LANES), 1)
    contract_last = (((1,), (1,)), ((), ()))
    for pair in range(D_ATT // LANES):
        cols = slice(pair * LANES, (pair + 1) * LANES)
        kb = k_ref[0, pl.ds(off, band), cols]
        vb = v_ref[0, pl.ds(off, band), cols]
        kc = kc_ref[0, :, cols]
        vc = vc_ref[0, :, cols]
        q = q_ref[0, :, cols]
        outs = []
        for hh in range(HEAD_PAIR):
            mine = (lane >= hh * HEAD_DIM) & (lane < (hh + 1) * HEAD_DIM)
            qm = jnp.where(mine, q, jnp.zeros_like(q))
            s_loc = lax.dot_general(qm, kb, contract_last, preferred_element_type=F32)
            s_ctx = lax.dot_general(qm, kc, contract_last, preferred_element_type=F32)
            s = jnp.concatenate([s_loc + tbl_ref[kind, pair * HEAD_PAIR + hh], s_ctx], axis=-1)
            e = jnp.exp(s - jnp.max(s, axis=-1, keepdims=True))
            denom = jnp.sum(e, axis=-1, keepdims=True)
            eb = e.astype(BF16)
            outs.append((_dot(eb[:, :band], vb) + _dot(eb[:, band:], vc)) / denom)
        o_ref[0, :, cols] = jnp.where(lane < HEAD_DIM, outs[0], outs[1]).astype(o_ref.dtype)


def _attention(q, k, v, k_ctx, v_ctx, tbl):
    b, t, _ = q.shape
    n_ctx = k_ctx.shape[1]
    rows = t // GRID_W
    tq = ATT_Q_ROWS * GRID_W
    full = lambda n: pl.BlockSpec((1, n, D_ATT), lambda bi, i: (bi, 0, 0))
    blk = pl.BlockSpec((1, tq, D_ATT), lambda bi, i: (bi, i, 0))
    return pl.pallas_call(
        functools.partial(_attn_kernel, rows=rows),
        grid=(b, rows // ATT_Q_ROWS),
        in_specs=[blk, full(t), full(t), full(n_ctx), full(n_ctx), _resident(tbl.shape)],
        out_specs=blk,
        out_shape=jax.ShapeDtypeStruct((b, t, D_ATT), BF16),
        compiler_params=pltpu.CompilerParams(
            dimension_semantics=("parallel", "arbitrary"), vmem_limit_bytes=VMEM_LIMIT_BYTES),
        name="attn",
    )(q, k, v, k_ctx, v_ctx, tbl)


def _bias_tables(rpb, rows):
    n_blk = rows // ATT_Q_ROWS
    n_off = 2 * WIN_ROWS - 1
    col = np.arange(GRID_W)
    col_start = np.clip(col - WIN_COLS // 2, 0, GRID_W - WIN_COLS)
    col_valid = (col[None, :] >= col_start[:, None]) & (col[None, :] < col_start[:, None] + WIN_COLS)
    col_off = np.clip(col[None, :] - col[:, None] + WIN_COLS - 1, 0, 2 * WIN_COLS - 2)
    pad = ATT_BAND_ROWS
    onehot_col = (col_off[:, :, None] == np.arange(2 * WIN_COLS - 1)).astype(np.float32)
    r_idx = np.arange(n_off + 2 * pad)[:, None, None] - pad + np.arange(2)[None, :, None]
    onehot_row = (r_idx == np.arange(n_off)[None, None, :]).astype(np.float32)
    pairs = jnp.einsum("hrj,Rer,qkj->hRqek", rpb, onehot_row, onehot_col, precision=lax.Precision.HIGHEST)
    pairs = pairs.reshape(N_ATT_HEADS, n_off + 2 * pad, GRID_W, 2 * GRID_W)
    slabs, valid = [], []
    for i in (0, 1, n_blk - 1):
        band_start = int(np.clip(ATT_Q_ROWS * i - WIN_ROWS // 2, 0, rows - ATT_BAND_ROWS))
        for rq in range(ATT_Q_ROWS):
            r = ATT_Q_ROWS * i + rq
            kr = band_start + np.arange(ATT_BAND_ROWS)
            win_start = int(np.clip(r - WIN_ROWS // 2, 0, rows - WIN_ROWS))
            row_valid = (kr >= win_start) & (kr < win_start + WIN_ROWS)
            first_off = band_start - r + WIN_ROWS - 1
            assert -pad <= first_off <= n_off
            lo = first_off + pad
            slabs.append(pairs[:, lo:lo + ATT_BAND_ROWS:2])
            valid.append(row_valid[None, :, None] & col_valid[:, None, :])
    bias = jnp.stack(slabs, axis=1)
    bias = bias.reshape(N_ATT_HEADS, 3, ATT_Q_ROWS, ATT_BAND_ROWS // 2, GRID_W, 2 * GRID_W)
    bias = bias.transpose(1, 0, 2, 4, 3, 5)
    bias = bias.reshape(3, N_ATT_HEADS, ATT_Q_ROWS * GRID_W, ATT_BAND_ROWS * GRID_W)
    mask = np.stack(valid).reshape(3, 1, ATT_Q_ROWS * GRID_W, ATT_BAND_ROWS * GRID_W)
    return jnp.where(mask, bias, NEG_INF)


def kernel(x, c, ctx, c_ctx, w_mod, b_mod, norm_pre, norm_post, ffn1_w_gu, ffn1_w_down, ffn2_w_gu,
           ffn2_w_down, w_in, w_out, conv_w, conv_b, lru_wa, lru_ba, lru_wx, lru_bx, lru_lambda, na_rpb):
    assert w_mod.shape[0] == 1, "single-layer problem"
    b, t, d = x.shape
    rows = t // GRID_W
    assert d == D_MODEL and t % (ATT_Q_ROWS * GRID_W) == 0 and rows >= 3 * ATT_Q_ROWS
    assert b + 1 <= MOD_ROWS

    c_rows = jnp.concatenate([c, c_ctx[None], jnp.zeros((MOD_ROWS - b - 1, d), F32)], axis=0)
    mod_all = _modulation(c_rows, w_mod[0], b_mod[0])
    mod = mod_all[:b].reshape(b, N_MOD, d)
    mod_ctx = jnp.broadcast_to(mod_all[b].reshape(1, N_MOD, d), (b, N_MOD, d))

    npre, npost = norm_pre[0], norm_post[0]
    wgu1, wd1 = ffn1_w_gu[0].astype(BF16), ffn1_w_down[0].astype(BF16)
    wgu2, wd2 = ffn2_w_gu[0].astype(BF16), ffn2_w_down[0].astype(BF16)
    win, wout = w_in[0].astype(BF16), w_out[0].astype(BF16)

    q_scale = HEAD_DIM ** -0.5
    x1, xr, gr, q, k, v = _layer_in(
        x, mod, npre, npost, wgu1, wd1, win,
        proj=((0, D_REC, 1.0, F32), (C_RG, D_REC, 1.0, F32), (C_Q, D_ATT, q_scale, BF16),
              (C_K, D_ATT, 1.0, BF16), (C_V, D_ATT, 1.0, BF16)),
        write_x=True)
    xr_ctx, k_ctx, v_ctx = _layer_in(
        ctx, mod_ctx, npre, npost, wgu1, wd1, win,
        proj=((0, D_REC, 1.0, F32), (C_K, D_ATT, 1.0, BF16), (C_V, D_ATT, 1.0, BF16)),
        write_x=False)

    wg, bg, lam = _rglru_params(lru_wa[0], lru_ba[0], lru_wx[0], lru_bx[0], lru_lambda[0])
    y_rec = _rglru(xr, xr_ctx, gr, conv_w[0], conv_b[0], wg, bg, lam)
    y_att = _attention(q, k, v, k_ctx, v_ctx, _bias_tables(na_rpb[0].astype(F32), rows))
    return _layer_out(x1, y_rec, y_att, mod, npre, npost, wout, wgu2, wd2)
```

```python
import functools

import numpy as np
import jax
import jax.numpy as jnp
from jax import lax
from jax.experimental import pallas as pl
from jax.experimental.pallas import tpu as pltpu

F32 = jnp.float32
BF16 = jnp.bfloat16

D_MODEL = 1024
D_REC = 512
D_ATT = 512
HEAD_DIM = 64
N_ATT_HEADS = 8
N_REC_BLOCKS = 8
REC_BLOCK = 64
CONV_W = 4
CONV_PAD_LEFT = 2
LRU_C = 8.0
GRID_W = 64
WIN_ROWS = 8
WIN_COLS = 16
D_FF = 2816
N_MOD = 9
EPS = 1e-6
NEG_INF = -1e30
C_RG, C_Q, C_K, C_V = 512, 1024, 1536, 2048
D_IN = 2560

LANES = 128
SUBLANES = 8
VMEM_LIMIT_BYTES = 56 * 1024 * 1024

TOKEN_TILE = 512
FF_CHUNK = 256
MOD_ROWS = 16
REC_GROUP = LANES
GATE_CHUNK = 512
SCAN_UNROLL = 16
ATT_Q_ROWS = 4
ATT_BAND_ROWS = 12
HEAD_PAIR = LANES // HEAD_DIM


def _sigmoid(x):
    return 0.5 * (jnp.tanh(0.5 * x) + 1.0)


def _rms(x, g):
    return x * lax.rsqrt(jnp.mean(x * x, axis=-1, keepdims=True) + EPS) * g


def _dot(a, b):
    return jnp.dot(a, b, preferred_element_type=F32)


def _resident(shape):
    return pl.BlockSpec(shape, lambda *_: (0,) * len(shape), pipeline_mode=pl.Buffered(1))


def _mod_kernel(c_ref, w_ref, b_ref, o_ref):
    c = c_ref[...]
    s = (c * _sigmoid(c)).astype(BF16)
    o_ref[...] = _dot(s, w_ref[...].astype(BF16)) + b_ref[...]


def _modulation(c_rows, w_mod, b_mod):
    n = w_mod.shape[1]
    return pl.pallas_call(
        _mod_kernel,
        grid=(n // D_MODEL,),
        in_specs=[
            pl.BlockSpec((MOD_ROWS, D_MODEL), lambda j: (0, 0)),
            pl.BlockSpec((D_MODEL, D_MODEL), lambda j: (0, j)),
            pl.BlockSpec((1, D_MODEL), lambda j: (0, j)),
        ],
        out_specs=pl.BlockSpec((MOD_ROWS, D_MODEL), lambda j: (0, j)),
        out_shape=jax.ShapeDtypeStruct((MOD_ROWS, n), F32),
        name="mod",
    )(c_rows, w_mod, b_mod.reshape(1, n))


def _swiglu(h, wgu_ref, wd_ref):
    acc = None
    for c in range(D_FF // FF_CHUNK):
        lo = c * FF_CHUNK
        g = _dot(h, wgu_ref[:, lo:lo + FF_CHUNK])
        u = _dot(h, wgu_ref[:, D_FF + lo:D_FF + lo + FF_CHUNK])
        a = (g * _sigmoid(g) * u).astype(BF16)
        d = _dot(a, wd_ref[lo:lo + FF_CHUNK, :])
        acc = d if acc is None else acc + d
    return acc


def _ffn_sublayer(x, mod_ref, npre_ref, npost_ref, idx, wgu_ref, wd_ref):
    shift = mod_ref[0, 3 * idx:3 * idx + 1, :]
    scale = mod_ref[0, 3 * idx + 1:3 * idx + 2, :]
    gate = mod_ref[0, 3 * idx + 2:3 * idx + 3, :]
    h = _rms(x, npre_ref[idx:idx + 1, :]) * (1.0 + scale) + shift
    y = _swiglu(h.astype(BF16), wgu_ref, wd_ref)
    return x + 0.5 * gate * _rms(y, npost_ref[idx:idx + 1, :])


def _layer_in_kernel(x_ref, mod_ref, npre_ref, npost_ref, wgu_ref, wd_ref, win_ref, *out_refs,
                     proj_cols, write_x):
    x1 = _ffn_sublayer(x_ref[0], mod_ref, npre_ref, npost_ref, 0, wgu_ref, wd_ref)
    outs = list(out_refs)
    if write_x:
        outs.pop(0)[0] = x1
    shift = mod_ref[0, 3:4, :]
    scale = mod_ref[0, 4:5, :]
    h = (_rms(x1, npre_ref[1:2, :]) * (1.0 + scale) + shift).astype(BF16)
    for o_ref, (lo, width, mult) in zip(outs, proj_cols):
        p = _dot(h, win_ref[:, lo:lo + width])
        if mult != 1.0:
            p = p * mult
        o_ref[0] = p.astype(o_ref.dtype)


def _layer_in(x, mod, npre, npost, wgu, wd, win, proj, write_x):
    b, t, d = x.shape
    tm = min(TOKEN_TILE, t)
    tok = lambda w: pl.BlockSpec((1, tm, w), lambda i, j: (i, j, 0))
    out_shape, out_specs = [], []
    if write_x:
        out_shape.append(jax.ShapeDtypeStruct((b, t, d), F32))
        out_specs.append(tok(d))
    for _, width, _, dtype in proj:
        out_shape.append(jax.ShapeDtypeStruct((b, t, width), dtype))
        out_specs.append(tok(width))
    return pl.pallas_call(
        functools.partial(_layer_in_kernel, proj_cols=tuple(p[:3] for p in proj), write_x=write_x),
        grid=(b, t // tm),
        in_specs=[
            tok(d),
            pl.BlockSpec((1, N_MOD, d), lambda i, j: (i, 0, 0)),
            _resident(npre.shape), _resident(npost.shape),
            _resident(wgu.shape), _resident(wd.shape), _resident(win.shape),
        ],
        out_specs=out_specs,
        out_shape=out_shape,
        compiler_params=pltpu.CompilerParams(
            dimension_semantics=("parallel", "parallel"), vmem_limit_bytes=VMEM_LIMIT_BYTES),
        name="layer_in",
    )(x, mod, npre, npost, wgu, wd, win)


def _layer_out_kernel(x_ref, yr_ref, ya_ref, mod_ref, npre_ref, npost_ref, wout_ref, wgu_ref, wd_ref,
                      o_ref):
    y = _dot(yr_ref[0], wout_ref[:D_REC, :]) + _dot(ya_ref[0], wout_ref[D_REC:, :])
    x2 = x_ref[0] + mod_ref[0, 5:6, :] * _rms(y, npost_ref[1:2, :])
    o_ref[0] = _ffn_sublayer(x2, mod_ref, npre_ref, npost_ref, 2, wgu_ref, wd_ref)


def _layer_out(x1, y_rec, y_att, mod, npre, npost, wout, wgu, wd):
    b, t, d = x1.shape
    tm = min(TOKEN_TILE, t)
    tok = lambda w: pl.BlockSpec((1, tm, w), lambda i, j: (i, j, 0))
    return pl.pallas_call(
        _layer_out_kernel,
        grid=(b, t // tm),
        in_specs=[
            tok(d), tok(D_REC), tok(D_ATT),
            pl.BlockSpec((1, N_MOD, d), lambda i, j: (i, 0, 0)),
            _resident(npre.shape), _resident(npost.shape),
            _resident(wout.shape), _resident(wgu.shape), _resident(wd.shape),
        ],
        out_specs=tok(d),
        out_shape=jax.ShapeDtypeStruct((b, t, d), F32),
        compiler_params=pltpu.CompilerParams(
            dimension_semantics=("parallel", "parallel"), vmem_limit_bytes=VMEM_LIMIT_BYTES),
        name="layer_out",
    )(x1, y_rec, y_att, mod, npre, npost, wout, wgu, wd)


def _scan_tile(a, u, carry, row, reverse):
    for s in (1, 2, 4):
        if reverse:
            keep = row < SUBLANES - s
            shift = SUBLANES - s
        else:
            keep = row >= s
            shift = s
        a_sh = jnp.where(keep, pltpu.roll(a, shift, 0), 1.0)
        u_sh = jnp.where(keep, pltpu.roll(u, shift, 0), 0.0)
        u = a * u_sh + u
        a = a * a_sh
    last = 0 if reverse else SUBLANES - 1
    return a * carry + u, a[last:last + 1, :] * carry + u[last:last + 1, :]


def _rglru_kernel(xr_ref, xc_ref, gr_ref, cw_ref, cb_ref, wg_ref, bg_ref, lam_ref, o_ref,
                  pad_s, conv_s, af_s, uf_s, ab_s, ub_s, hf_s, hb_s):
    cg = REC_GROUP
    lam = lam_ref[0]
    log_sig_lam = jnp.minimum(lam, 0.0) - jnp.log(1.0 + jnp.exp(-jnp.abs(lam)))
    neg_half_c_lsl = (-0.5 * LRU_C) * log_sig_lam
    row = lax.broadcasted_iota(jnp.int32, (SUBLANES, cg), 0)

    def conv_gates(src_ref, n):
        halo = SUBLANES
        pad_s[0:halo, :] = jnp.zeros((halo, cg), F32)
        pad_s[halo:halo + n, :] = src_ref[0]
        pad_s[halo + n:2 * halo + n, :] = jnp.zeros((halo, cg), F32)
        y = cb_ref[...]
        for k in range(CONV_W):
            off = halo + k - CONV_PAD_LEFT
            y = y + pad_s[off:off + n, :] * cw_ref[k:k + 1, :]
        conv_s[0:n, :] = y

        chunk = min(GATE_CHUNK, n)

        def gate_body(ci, _):
            r0 = pl.multiple_of(ci * chunk, chunk)
            xv = conv_s[pl.ds(r0, chunk), :]
            g = _dot(xv.astype(BF16), wg_ref[0]) + bg_ref[0]
            x_half = 0.5 * xv
            for d, (a_s, u_s) in enumerate(((af_s, uf_s), (ab_s, ub_s))):
                ta = jnp.tanh(g[:, (2 * d) * cg:(2 * d + 1) * cg])
                tx = jnp.tanh(g[:, (2 * d + 1) * cg:(2 * d + 2) * cg])
                neg_log_a = neg_half_c_lsl[:, d * cg:(d + 1) * cg] * (ta + 1.0)
                a = jnp.exp(-neg_log_a)
                a_s[pl.ds(r0, chunk), :] = a
                one_minus_a2 = jnp.tanh(neg_log_a) * (a * a + 1.0)
                root = jnp.where(one_minus_a2 > 0.0, one_minus_a2 * lax.rsqrt(one_minus_a2), 0.0)
                u_s[pl.ds(r0, chunk), :] = root * (tx + 1.0) * x_half
            return 0

        lax.fori_loop(0, n // chunk, gate_body, 0)

    def scan(n, hf0, hb0):
        tiles = n // SUBLANES

        def body(k, carry):
            hf, hb = carry
            rf = pl.multiple_of(k * SUBLANES, SUBLANES)
            rb = pl.multiple_of((tiles - 1 - k) * SUBLANES, SUBLANES)
            h, hf = _scan_tile(af_s[pl.ds(rf, SUBLANES), :], uf_s[pl.ds(rf, SUBLANES), :], hf, row, False)
            hf_s[pl.ds(rf, SUBLANES), :] = h
            h, hb = _scan_tile(ab_s[pl.ds(rb, SUBLANES), :], ub_s[pl.ds(rb, SUBLANES), :], hb, row, True)
            hb_s[pl.ds(rb, SUBLANES), :] = h
            return hf, hb

        return lax.fori_loop(0, tiles, body, (hf0, hb0), unroll=SCAN_UNROLL)

    n_ctx = xc_ref.shape[1]
    n_lat = xr_ref.shape[1]
    zero = jnp.zeros((1, cg), F32)
    conv_gates(xc_ref, n_ctx)
    hf, hb = scan(n_ctx, zero, zero)
    conv_gates(xr_ref, n_lat)
    scan(n_lat, hf, hb)
    y = hf_s[0:n_lat, :] + hb_s[0:n_lat, :]
    o_ref[0] = (y * jax.nn.gelu(gr_ref[0])).astype(o_ref.dtype)


def _rglru(xr, xr_ctx, gr, conv_w, conv_b, wg, bg, lam):
    b, t, _ = xr.shape
    n_ctx = xr_ctx.shape[1]
    cg = REC_GROUP
    seq = lambda n: pl.BlockSpec((1, n, cg), lambda i, j: (i, 0, j))
    grp = lambda shape: pl.BlockSpec((1,) + shape, lambda i, j: (j, 0, 0))
    return pl.pallas_call(
        _rglru_kernel,
        grid=(b, D_REC // cg),
        in_specs=[
            seq(t), seq(n_ctx), seq(t),
            pl.BlockSpec((CONV_W, cg), lambda i, j: (0, j)),
            pl.BlockSpec((1, cg), lambda i, j: (0, j)),
            grp((cg, 4 * cg)), grp((1, 4 * cg)), grp((1, 2 * cg)),
        ],
        out_specs=seq(t),
        out_shape=jax.ShapeDtypeStruct((b, t, D_REC), BF16),
        scratch_shapes=[pltpu.VMEM((t + 2 * SUBLANES, cg), F32)] + [pltpu.VMEM((t, cg), F32)] * 7,
        compiler_params=pltpu.CompilerParams(
            dimension_semantics=("parallel", "parallel"), vmem_limit_bytes=VMEM_LIMIT_BYTES),
        name="rglru",
    )(xr, xr_ctx, gr, conv_w, conv_b.reshape(1, D_REC), wg, bg, lam)


def _rglru_params(lru_wa, lru_ba, lru_wx, lru_bx, lru_lambda):
    cg = REC_GROUP
    n_grp = D_REC // cg
    per = cg // REC_BLOCK

    def block_diag(w):
        w = w.reshape(n_grp, per, REC_BLOCK, REC_BLOCK)
        eye = jnp.eye(per, dtype=w.dtype)
        return jnp.einsum("gpcd,pq->gpcqd", w, eye).reshape(n_grp, cg, cg)

    wg = (0.5 * jnp.concatenate([block_diag(lru_wa[0]), block_diag(lru_wx[0]),
                                 block_diag(lru_wa[1]), block_diag(lru_wx[1])], axis=-1)).astype(BF16)
    vec = lambda v: v.reshape(n_grp, 1, cg)
    bg = 0.5 * jnp.concatenate([vec(lru_ba[0]), vec(lru_bx[0]), vec(lru_ba[1]), vec(lru_bx[1])], axis=-1)
    lam = jnp.concatenate([vec(lru_lambda[0]), vec(lru_lambda[1])], axis=-1)
    return wg, bg, lam


def _attn_kernel(q_ref, k_ref, v_ref, kc_ref, vc_ref, tbl_ref, o_ref, *, rows):
    i = pl.program_id(1)
    n_blk = pl.num_programs(1)
    start = jnp.clip(ATT_Q_ROWS * i - WIN_ROWS // 2, 0, rows - ATT_BAND_ROWS)
    kind = jnp.where(i == 0, 0, jnp.where(i == n_blk - 1, 2, 1))
    off = pl.multiple_of(start * GRID_W, GRID_W)
    band = ATT_BAND_ROWS * GRID_W
    lane = lax.broadcasted_iota(jnp.int32, (ATT_Q_ROWS * GRID_W, LANES), 1)
    contract_last = (((1,), (1,)), ((), ()))
    for pair in range(D_ATT // LANES):
        cols = slice(pair * LANES, (pair + 1) * LANES)
        kb = k_ref[0, pl.ds(off, band), cols]
        vb = v_ref[0, pl.ds(off, band), cols]
        kc = kc_ref[0, :, cols]
        vc = vc_ref[0, :, cols]
        q = q_ref[0, :, cols]
        outs = []
        for hh in range(HEAD_PAIR):
            mine = (lane >= hh * HEAD_DIM) & (lane < (hh + 1) * HEAD_DIM)
            qm = jnp.where(mine, q, jnp.zeros_like(q))
            s_loc = lax.dot_general(qm, kb, contract_last, preferred_element_type=F32)
            s_ctx = lax.dot_general(qm, kc, contract_last, preferred_element_type=F32)
            s = jnp.concatenate([s_loc + tbl_ref[kind, pair * HEAD_PAIR + hh], s_ctx], axis=-1)
            e = jnp.exp(s - jnp.max(s, axis=-1, keepdims=True))
            denom = jnp.sum(e, axis=-1, keepdims=True)
            eb = e.astype(BF16)
            outs.append((_dot(eb[:, :band], vb) + _dot(eb[:, band:], vc)) / denom)
        o_ref[0, :, cols] = jnp.where(lane < HEAD_DIM, outs[0], outs[1]).astype(o_ref.dtype)


def _attention(q, k, v, k_ctx, v_ctx, tbl):
    b, t, _ = q.shape
    n_ctx = k_ctx.shape[1]
    rows = t // GRID_W
    tq = ATT_Q_ROWS * GRID_W
    full = lambda n: pl.BlockSpec((1, n, D_ATT), lambda bi, i: (bi, 0, 0))
    blk = pl.BlockSpec((1, tq, D_ATT), lambda bi, i: (bi, i, 0))
    return pl.pallas_call(
        functools.partial(_attn_kernel, rows=rows),
        grid=(b, rows // ATT_Q_ROWS),
        in_specs=[blk, full(t), full(t), full(n_ctx), full(n_ctx), _resident(tbl.shape)],
        out_specs=blk,
        out_shape=jax.ShapeDtypeStruct((b, t, D_ATT), BF16),
        compiler_params=pltpu.CompilerParams(
            dimension_semantics=("parallel", "arbitrary"), vmem_limit_bytes=VMEM_LIMIT_BYTES),
        name="attn",
    )(q, k, v, k_ctx, v_ctx, tbl)


def _bias_tables(rpb, rows):
    n_blk = rows // ATT_Q_ROWS
    n_off = 2 * WIN_ROWS - 1
    pad = ATT_BAND_ROWS
    n_pairs = n_off + 2 * pad
    col = np.arange(GRID_W)
    col_off = np.clip(col[None, :] - col[:, None] + WIN_COLS - 1, 0, 2 * WIN_COLS - 2)
    n_col_off = 2 * WIN_COLS - 1
    rp = jnp.pad(rpb, ((0, 0), (pad, pad + 1), (0, 0)))
    z = jnp.concatenate([rp[:, :n_pairs], rp[:, 1:n_pairs + 1]], axis=-1)
    m = np.arange(2 * GRID_W)
    f = np.arange(2 * n_col_off)
    select = ((f[:, None, None] // n_col_off == m[None, None, :] // GRID_W)
              & (f[:, None, None] % n_col_off == col_off[:, m % GRID_W][None])).astype(np.float32)
    pairs = jnp.einsum("hRf,fqm->hRqm", z, select, precision=lax.Precision.HIGHEST)

    plan = []
    for i in (0, 1, n_blk - 1):
        band_start = int(np.clip(ATT_Q_ROWS * i - WIN_ROWS // 2, 0, rows - ATT_BAND_ROWS))
        per_row = []
        for rq in range(ATT_Q_ROWS):
            r = ATT_Q_ROWS * i + rq
            kr = band_start + np.arange(ATT_BAND_ROWS)
            win_start = int(np.clip(r - WIN_ROWS // 2, 0, rows - WIN_ROWS))
            row_valid = tuple(bool(ok) for ok in (kr >= win_start) & (kr < win_start + WIN_ROWS))
            first_off = band_start - r + WIN_ROWS - 1
            assert -pad <= first_off <= n_off
            per_row.append((first_off + pad, row_valid))
        plan.append(tuple(per_row))

    tq, band = ATT_Q_ROWS * GRID_W, ATT_BAND_ROWS * GRID_W
    return pl.pallas_call(
        functools.partial(_table_kernel, plan=tuple(plan)),
        grid=(N_ATT_HEADS,),
        in_specs=[pl.BlockSpec((1, n_pairs, GRID_W, 2 * GRID_W), lambda h: (h, 0, 0, 0))],
        out_specs=pl.BlockSpec((3, 1, tq, band), lambda h: (0, h, 0, 0)),
        out_shape=jax.ShapeDtypeStruct((3, N_ATT_HEADS, tq, band), F32),
        name="bias_table",
    )(pairs)


def _table_kernel(pairs_ref, o_ref, *, plan):
    qc = lax.broadcasted_iota(jnp.int32, (GRID_W, 2 * GRID_W), 0)
    m = lax.broadcasted_iota(jnp.int32, (GRID_W, 2 * GRID_W), 1)
    kc = jnp.where(m < GRID_W, m, m - GRID_W)
    col_start = jnp.clip(qc - WIN_COLS // 2, 0, GRID_W - WIN_COLS)
    col_ok = (kc >= col_start) & (kc < col_start + WIN_COLS)
    first_half = m < GRID_W
    masked = jnp.full((GRID_W, 2 * GRID_W), NEG_INF, F32)
    for kind, per_row in enumerate(plan):
        for rq, (lo, row_valid) in enumerate(per_row):
            for jp in range(ATT_BAND_ROWS // 2):
                ok0, ok1 = row_valid[2 * jp], row_valid[2 * jp + 1]
                if ok0 and ok1:
                    ok = col_ok
                elif ok0:
                    ok = col_ok & first_half
                elif ok1:
                    ok = col_ok & jnp.logical_not(first_half)
                else:
                    ok = None
                tile = masked if ok is None else jnp.where(ok, pairs_ref[0, lo + 2 * jp], NEG_INF)
                o_ref[kind, 0, rq * GRID_W:(rq + 1) * GRID_W, jp * 2 * GRID_W:(jp + 1) * 2 * GRID_W] = tile


def kernel(x, c, ctx, c_ctx, w_mod, b_mod, norm_pre, norm_post, ffn1_w_gu, ffn1_w_down, ffn2_w_gu,
           ffn2_w_down, w_in, w_out, conv_w, conv_b, lru_wa, lru_ba, lru_wx, lru_bx, lru_lambda, na_rpb):
    assert w_mod.shape[0] == 1, "single-layer problem"
    b, t, d = x.shape
    rows = t // GRID_W
    assert d == D_MODEL and t % (ATT_Q_ROWS * GRID_W) == 0 and rows >= 3 * ATT_Q_ROWS
    assert b + 1 <= MOD_ROWS

    c_rows = jnp.concatenate([c, c_ctx[None], jnp.zeros((MOD_ROWS - b - 1, d), F32)], axis=0)
    mod_all = _modulation(c_rows, w_mod[0], b_mod[0])
    mod = mod_all[:b].reshape(b, N_MOD, d)
    mod_ctx = jnp.broadcast_to(mod_all[b].reshape(1, N_MOD, d), (b, N_MOD, d))

    npre, npost = norm_pre[0], norm_post[0]
    wgu1, wd1 = ffn1_w_gu[0].astype(BF16), ffn1_w_down[0].astype(BF16)
    wgu2, wd2 = ffn2_w_gu[0].astype(BF16), ffn2_w_down[0].astype(BF16)
    win, wout = w_in[0].astype(BF16), w_out[0].astype(BF16)

    q_scale = HEAD_DIM ** -0.5
    x1, xr, gr, q, k, v = _layer_in(
        x, mod, npre, npost, wgu1, wd1, win,
        proj=((0, D_REC, 1.0, F32), (C_RG, D_REC, 1.0, F32), (C_Q, D_ATT, q_scale, BF16),
              (C_K, D_ATT, 1.0, BF16), (C_V, D_ATT, 1.0, BF16)),
        write_x=True)
    xr_ctx, k_ctx, v_ctx = _layer_in(
        ctx, mod_ctx, npre, npost, wgu1, wd1, win,
        proj=((0, D_REC, 1.0, F32), (C_K, D_ATT, 1.0, BF16), (C_V, D_ATT, 1.0, BF16)),
        write_x=False)

    wg, bg, lam = _rglru_params(lru_wa[0], lru_ba[0], lru_wx[0], lru_bx[0], lru_lambda[0])
    y_rec = _rglru(xr, xr_ctx, gr, conv_w[0], conv_b[0], wg, bg, lam)
    y_att = _attention(q, k, v, k_ctx, v_ctx, _bias_tables(na_rpb[0].astype(F32), rows))
    return _layer_out(x1, y_rec, y_att, mod, npre, npost, wout, wgu2, wd2)
```

```python
import functools

import numpy as np
import jax
import jax.numpy as jnp
from jax import lax
from jax.experimental import pallas as pl
from jax.experimental.pallas import tpu as pltpu

F32 = jnp.float32
BF16 = jnp.bfloat16

D_MODEL = 1024
D_REC = 512
D_ATT = 512
HEAD_DIM = 64
N_ATT_HEADS = 8
N_REC_BLOCKS = 8
REC_BLOCK = 64
CONV_W = 4
CONV_PAD_LEFT = 2
LRU_C = 8.0
GRID_W = 64
WIN_ROWS = 8
WIN_COLS = 16
D_FF = 2816
N_MOD = 9
EPS = 1e-6
NEG_INF = -1e30
C_RG, C_Q, C_K, C_V = 512, 1024, 1536, 2048
D_IN = 2560

LANES = 128
SUBLANES = 8
VMEM_LIMIT_BYTES = 56 * 1024 * 1024

TOKEN_TILE = 512
FF_CHUNK = 256
MOD_ROWS = 16
REC_GROUP = LANES
GATE_CHUNK = 512
SCAN_UNROLL = 16
ATT_Q_ROWS = 4
ATT_BAND_ROWS = 12
HEAD_PAIR = LANES // HEAD_DIM
ATT_ROW_CHUNK = 32


def _sigmoid(x):
    return 0.5 * (jnp.tanh(0.5 * x) + 1.0)


def _rms(x, g):
    return x * lax.rsqrt(jnp.mean(x * x, axis=-1, keepdims=True) + EPS) * g


def _dot(a, b):
    return jnp.dot(a, b, preferred_element_type=F32)


def _resident(shape):
    return pl.BlockSpec(shape, lambda *_: (0,) * len(shape), pipeline_mode=pl.Buffered(1))


def _mod_kernel(c_ref, w_ref, b_ref, o_ref):
    c = c_ref[...]
    s = (c * _sigmoid(c)).astype(BF16)
    o_ref[...] = _dot(s, w_ref[...].astype(BF16)) + b_ref[...]


def _modulation(c_rows, w_mod, b_mod):
    n = w_mod.shape[1]
    return pl.pallas_call(
        _mod_kernel,
        grid=(n // D_MODEL,),
        in_specs=[
            pl.BlockSpec((MOD_ROWS, D_MODEL), lambda j: (0, 0)),
            pl.BlockSpec((D_MODEL, D_MODEL), lambda j: (0, j)),
            pl.BlockSpec((1, D_MODEL), lambda j: (0, j)),
        ],
        out_specs=pl.BlockSpec((MOD_ROWS, D_MODEL), lambda j: (0, j)),
        out_shape=jax.ShapeDtypeStruct((MOD_ROWS, n), F32),
        name="mod",
    )(c_rows, w_mod, b_mod.reshape(1, n))


def _swiglu(h, wgu_ref, wd_ref):
    acc = None
    for c in range(D_FF // FF_CHUNK):
        lo = c * FF_CHUNK
        g = _dot(h, wgu_ref[:, lo:lo + FF_CHUNK])
        u = _dot(h, wgu_ref[:, D_FF + lo:D_FF + lo + FF_CHUNK])
        a = (g * _sigmoid(g) * u).astype(BF16)
        d = _dot(a, wd_ref[lo:lo + FF_CHUNK, :])
        acc = d if acc is None else acc + d
    return acc


def _ffn_sublayer(x, mod_ref, npre_ref, npost_ref, idx, wgu_ref, wd_ref):
    shift = mod_ref[0, 3 * idx:3 * idx + 1, :]
    scale = mod_ref[0, 3 * idx + 1:3 * idx + 2, :]
    gate = mod_ref[0, 3 * idx + 2:3 * idx + 3, :]
    h = _rms(x, npre_ref[idx:idx + 1, :]) * (1.0 + scale) + shift
    y = _swiglu(h.astype(BF16), wgu_ref, wd_ref)
    return x + 0.5 * gate * _rms(y, npost_ref[idx:idx + 1, :])


def _layer_in_kernel(x_ref, mod_ref, npre_ref, npost_ref, wgu_ref, wd_ref, win_ref, *out_refs,
                     proj_cols, write_x):
    x1 = _ffn_sublayer(x_ref[0], mod_ref, npre_ref, npost_ref, 0, wgu_ref, wd_ref)
    outs = list(out_refs)
    if write_x:
        outs.pop(0)[0] = x1
    shift = mod_ref[0, 3:4, :]
    scale = mod_ref[0, 4:5, :]
    h = (_rms(x1, npre_ref[1:2, :]) * (1.0 + scale) + shift).astype(BF16)
    for o_ref, (lo, width, mult) in zip(outs, proj_cols):
        p = _dot(h, win_ref[:, lo:lo + width])
        if mult != 1.0:
            p = p * mult
        o_ref[0] = p.astype(o_ref.dtype)


def _layer_in(x, mod, npre, npost, wgu, wd, win, proj, write_x):
    b, t, d = x.shape
    tm = min(TOKEN_TILE, t)
    tok = lambda w: pl.BlockSpec((1, tm, w), lambda i, j: (i, j, 0))
    out_shape, out_specs = [], []
    if write_x:
        out_shape.append(jax.ShapeDtypeStruct((b, t, d), F32))
        out_specs.append(tok(d))
    for _, width, _, dtype in proj:
        out_shape.append(jax.ShapeDtypeStruct((b, t, width), dtype))
        out_specs.append(tok(width))
    return pl.pallas_call(
        functools.partial(_layer_in_kernel, proj_cols=tuple(p[:3] for p in proj), write_x=write_x),
        grid=(b, t // tm),
        in_specs=[
            tok(d),
            pl.BlockSpec((1, N_MOD, d), lambda i, j: (i, 0, 0)),
            _resident(npre.shape), _resident(npost.shape),
            _resident(wgu.shape), _resident(wd.shape), _resident(win.shape),
        ],
        out_specs=out_specs,
        out_shape=out_shape,
        compiler_params=pltpu.CompilerParams(
            dimension_semantics=("parallel", "parallel"), vmem_limit_bytes=VMEM_LIMIT_BYTES),
        name="layer_in",
    )(x, mod, npre, npost, wgu, wd, win)


def _layer_out_kernel(x_ref, yr_ref, ya_ref, mod_ref, npre_ref, npost_ref, wout_ref, wgu_ref, wd_ref,
                      o_ref):
    y = _dot(yr_ref[0], wout_ref[:D_REC, :]) + _dot(ya_ref[0], wout_ref[D_REC:, :])
    x2 = x_ref[0] + mod_ref[0, 5:6, :] * _rms(y, npost_ref[1:2, :])
    o_ref[0] = _ffn_sublayer(x2, mod_ref, npre_ref, npost_ref, 2, wgu_ref, wd_ref)


def _layer_out(x1, y_rec, y_att, mod, npre, npost, wout, wgu, wd):
    b, t, d = x1.shape
    tm = min(TOKEN_TILE, t)
    tok = lambda w: pl.BlockSpec((1, tm, w), lambda i, j: (i, j, 0))
    return pl.pallas_call(
        _layer_out_kernel,
        grid=(b, t // tm),
        in_specs=[
            tok(d), tok(D_REC), tok(D_ATT),
            pl.BlockSpec((1, N_MOD, d), lambda i, j: (i, 0, 0)),
            _resident(npre.shape), _resident(npost.shape),
            _resident(wout.shape), _resident(wgu.shape), _resident(wd.shape),
        ],
        out_specs=tok(d),
        out_shape=jax.ShapeDtypeStruct((b, t, d), F32),
        compiler_params=pltpu.CompilerParams(
            dimension_semantics=("parallel", "parallel"), vmem_limit_bytes=VMEM_LIMIT_BYTES),
        name="layer_out",
    )(x1, y_rec, y_att, mod, npre, npost, wout, wgu, wd)


def _scan_tile(a, u, carry, row, reverse):
    for s in (1, 2, 4):
        if reverse:
            keep = row < SUBLANES - s
            shift = SUBLANES - s
        else:
            keep = row >= s
            shift = s
        a_sh = jnp.where(keep, pltpu.roll(a, shift, 0), 1.0)
        u_sh = jnp.where(keep, pltpu.roll(u, shift, 0), 0.0)
        u = a * u_sh + u
        a = a * a_sh
    last = 0 if reverse else SUBLANES - 1
    return a * carry + u, a[last:last + 1, :] * carry + u[last:last + 1, :]


def _rglru_kernel(xr_ref, xc_ref, gr_ref, cw_ref, cb_ref, wg_ref, bg_ref, lam_ref, o_ref,
                  pad_s, conv_s, af_s, uf_s, ab_s, ub_s, hf_s, hb_s):
    cg = REC_GROUP
    lam = lam_ref[0]
    log_sig_lam = jnp.minimum(lam, 0.0) - jnp.log(1.0 + jnp.exp(-jnp.abs(lam)))
    neg_half_c_lsl = (-0.5 * LRU_C) * log_sig_lam
    row = lax.broadcasted_iota(jnp.int32, (SUBLANES, cg), 0)

    def conv_gates(src_ref, n):
        halo = SUBLANES
        pad_s[0:halo, :] = jnp.zeros((halo, cg), F32)
        pad_s[halo:halo + n, :] = src_ref[0]
        pad_s[halo + n:2 * halo + n, :] = jnp.zeros((halo, cg), F32)
        y = cb_ref[...]
        for k in range(CONV_W):
            off = halo + k - CONV_PAD_LEFT
            y = y + pad_s[off:off + n, :] * cw_ref[k:k + 1, :]
        conv_s[0:n, :] = y

        chunk = min(GATE_CHUNK, n)

        def gate_body(ci, _):
            r0 = pl.multiple_of(ci * chunk, chunk)
            xv = conv_s[pl.ds(r0, chunk), :]
            g = _dot(xv.astype(BF16), wg_ref[0]) + bg_ref[0]
            x_half = 0.5 * xv
            for d, (a_s, u_s) in enumerate(((af_s, uf_s), (ab_s, ub_s))):
                ta = jnp.tanh(g[:, (2 * d) * cg:(2 * d + 1) * cg])
                tx = jnp.tanh(g[:, (2 * d + 1) * cg:(2 * d + 2) * cg])
                neg_log_a = neg_half_c_lsl[:, d * cg:(d + 1) * cg] * (ta + 1.0)
                a = jnp.exp(-neg_log_a)
                a_s[pl.ds(r0, chunk), :] = a
                one_minus_a2 = jnp.tanh(neg_log_a) * (a * a + 1.0)
                root = jnp.where(one_minus_a2 > 0.0, one_minus_a2 * lax.rsqrt(one_minus_a2), 0.0)
                u_s[pl.ds(r0, chunk), :] = root * (tx + 1.0) * x_half
            return 0

        lax.fori_loop(0, n // chunk, gate_body, 0)

    def scan(n, hf0, hb0):
        tiles = n // SUBLANES

        def body(k, carry):
            hf, hb = carry
            rf = pl.multiple_of(k * SUBLANES, SUBLANES)
            rb = pl.multiple_of((tiles - 1 - k) * SUBLANES, SUBLANES)
            h, hf = _scan_tile(af_s[pl.ds(rf, SUBLANES), :], uf_s[pl.ds(rf, SUBLANES), :], hf, row, False)
            hf_s[pl.ds(rf, SUBLANES), :] = h
            h, hb = _scan_tile(ab_s[pl.ds(rb, SUBLANES), :], ub_s[pl.ds(rb, SUBLANES), :], hb, row, True)
            hb_s[pl.ds(rb, SUBLANES), :] = h
            return hf, hb

        return lax.fori_loop(0, tiles, body, (hf0, hb0), unroll=SCAN_UNROLL)

    n_ctx = xc_ref.shape[1]
    n_lat = xr_ref.shape[1]
    zero = jnp.zeros((1, cg), F32)
    conv_gates(xc_ref, n_ctx)
    hf, hb = scan(n_ctx, zero, zero)
    conv_gates(xr_ref, n_lat)
    scan(n_lat, hf, hb)
    y = hf_s[0:n_lat, :] + hb_s[0:n_lat, :]
    o_ref[0] = (y * jax.nn.gelu(gr_ref[0])).astype(o_ref.dtype)


def _rglru(xr, xr_ctx, gr, conv_w, conv_b, wg, bg, lam):
    b, t, _ = xr.shape
    n_ctx = xr_ctx.shape[1]
    cg = REC_GROUP
    seq = lambda n: pl.BlockSpec((1, n, cg), lambda i, j: (i, 0, j))
    grp = lambda shape: pl.BlockSpec((1,) + shape, lambda i, j: (j, 0, 0))
    return pl.pallas_call(
        _rglru_kernel,
        grid=(b, D_REC // cg),
        in_specs=[
            seq(t), seq(n_ctx), seq(t),
            pl.BlockSpec((CONV_W, cg), lambda i, j: (0, j)),
            pl.BlockSpec((1, cg), lambda i, j: (0, j)),
            grp((cg, 4 * cg)), grp((1, 4 * cg)), grp((1, 2 * cg)),
        ],
        out_specs=seq(t),
        out_shape=jax.ShapeDtypeStruct((b, t, D_REC), BF16),
        scratch_shapes=[pltpu.VMEM((t + 2 * SUBLANES, cg), F32)] + [pltpu.VMEM((t, cg), F32)] * 7,
        compiler_params=pltpu.CompilerParams(
            dimension_semantics=("parallel", "parallel"), vmem_limit_bytes=VMEM_LIMIT_BYTES),
        name="rglru",
    )(xr, xr_ctx, gr, conv_w, conv_b.reshape(1, D_REC), wg, bg, lam)


def _rglru_params(lru_wa, lru_ba, lru_wx, lru_bx, lru_lambda):
    cg = REC_GROUP
    n_grp = D_REC // cg
    per = cg // REC_BLOCK

    def block_diag(w):
        w = w.reshape(n_grp, per, REC_BLOCK, REC_BLOCK)
        eye = jnp.eye(per, dtype=w.dtype)
        return jnp.einsum("gpcd,pq->gpcqd", w, eye).reshape(n_grp, cg, cg)

    wg = (0.5 * jnp.concatenate([block_diag(lru_wa[0]), block_diag(lru_wx[0]),
                                 block_diag(lru_wa[1]), block_diag(lru_wx[1])], axis=-1)).astype(BF16)
    vec = lambda v: v.reshape(n_grp, 1, cg)
    bg = 0.5 * jnp.concatenate([vec(lru_ba[0]), vec(lru_bx[0]), vec(lru_ba[1]), vec(lru_bx[1])], axis=-1)
    lam = jnp.concatenate([vec(lru_lambda[0]), vec(lru_lambda[1])], axis=-1)
    return wg, bg, lam


def _attn_kernel(q_ref, k_ref, v_ref, kc_ref, vc_ref, tbl_ref, o_ref, s_buf, p_buf, *, rows):
    i = pl.program_id(1)
    n_blk = pl.num_programs(1)
    start = jnp.clip(ATT_Q_ROWS * i - WIN_ROWS // 2, 0, rows - ATT_BAND_ROWS)
    kind = jnp.where(i == 0, 0, jnp.where(i == n_blk - 1, 2, 1))
    off = pl.multiple_of(start * GRID_W, GRID_W)
    band = ATT_BAND_ROWS * GRID_W
    tq = ATT_Q_ROWS * GRID_W
    lane = lax.broadcasted_iota(jnp.int32, (tq, LANES), 1)
    contract_last = (((1,), (1,)), ((), ()))

    def head_cols(head):
        pair = head // HEAD_PAIR
        return slice(pair * LANES, (pair + 1) * LANES)

    def scores(head):
        cols, hh, slot = head_cols(head), head % HEAD_PAIR, head % 2
        q = q_ref[0, :, cols]
        mine = (lane >= hh * HEAD_DIM) & (lane < (hh + 1) * HEAD_DIM)
        qm = jnp.where(mine, q, jnp.zeros_like(q))
        s_loc = lax.dot_general(qm, k_ref[0, pl.ds(off, band), cols], contract_last,
                                preferred_element_type=F32)
        s_buf[slot, :, :band] = s_loc
        s_buf[slot, :, band:] = lax.dot_general(qm, kc_ref[0, :, cols], contract_last,
                                                preferred_element_type=F32)

    def softmax(head):
        slot = head % 2
        sums = []
        for r0 in range(0, tq, ATT_ROW_CHUNK):
            s = jnp.concatenate(
                [s_buf[slot, r0:r0 + ATT_ROW_CHUNK, :band] + tbl_ref[kind, head, r0:r0 + ATT_ROW_CHUNK, :],
                 s_buf[slot, r0:r0 + ATT_ROW_CHUNK, band:]], axis=-1)
            e =jnp.exp(s - jnp.max(s, axis=-1, keepdims=True))
            sums.append(jnp.sum(e, axis=-1, keepdims=True))
            p_buf[slot, r0:r0 + ATT_ROW_CHUNK, :] = e.astype(BF16)
        return jnp.concatenate(sums, axis=0)

    def values(head, denom):
        cols, slot = head_cols(head), head % 2
        o = (_dot(p_buf[slot, :, :band], v_ref[0, pl.ds(off, band), cols])
             + _dot(p_buf[slot, :, band:], vc_ref[0, :, cols]))
        return o / denom

    scores(0)
    outs = []
    for head in range(N_ATT_HEADS):
        if head + 1 < N_ATT_HEADS:
            scores(head + 1)
        outs.append(values(head, softmax(head)))
        if head % HEAD_PAIR == HEAD_PAIR - 1:
            o_ref[0, :, head_cols(head)] = jnp.where(lane < HEAD_DIM, outs[-2], outs[-1]).astype(o_ref.dtype)


def _attention(q, k, v, k_ctx, v_ctx, tbl):
    b, t, _ = q.shape
    n_ctx = k_ctx.shape[1]
    rows = t // GRID_W
    tq = ATT_Q_ROWS * GRID_W
    n_keys = ATT_BAND_ROWS * GRID_W + n_ctx
    full = lambda n: pl.BlockSpec((1, n, D_ATT), lambda bi, i: (bi, 0, 0))
    blk = pl.BlockSpec((1, tq, D_ATT), lambda bi, i: (bi, i, 0))
    return pl.pallas_call(
        functools.partial(_attn_kernel, rows=rows),
        grid=(b, rows // ATT_Q_ROWS),
        in_specs=[blk, full(t), full(t), full(n_ctx), full(n_ctx), _resident(tbl.shape)],
        out_specs=blk,
        out_shape=jax.ShapeDtypeStruct((b, t, D_ATT), BF16),
        scratch_shapes=[pltpu.VMEM((2, tq, n_keys), F32), pltpu.VMEM((2, tq, n_keys), BF16)],
        compiler_params=pltpu.CompilerParams(
            dimension_semantics=("parallel", "arbitrary"), vmem_limit_bytes=VMEM_LIMIT_BYTES),
        name="attn",
    )(q, k, v, k_ctx, v_ctx, tbl)


def _bias_tables(rpb, rows):
    n_blk = rows // ATT_Q_ROWS
    n_off = 2 * WIN_ROWS - 1
    pad = ATT_BAND_ROWS
    n_pairs = n_off + 2 * pad
    col = np.arange(GRID_W)
    col_off = np.clip(col[None, :] - col[:, None] + WIN_COLS - 1, 0, 2 * WIN_COLS - 2)
    n_col_off = 2 * WIN_COLS - 1
    rp = jnp.pad(rpb, ((0, 0), (pad, pad + 1), (0, 0)))
    z = jnp.concatenate([rp[:, :n_pairs], rp[:, 1:n_pairs + 1]], axis=-1)
    m = np.arange(2 * GRID_W)
    f = np.arange(2 * n_col_off)
    select = ((f[:, None, None] // n_col_off == m[None, None, :] // GRID_W)
              & (f[:, None, None] % n_col_off == col_off[:, m % GRID_W][None])).astype(np.float32)
    pairs = jnp.einsum("hRf,fqm->hRqm", z, select, precision=lax.Precision.HIGHEST)

    plan = []
    for i in (0, 1, n_blk - 1):
        band_start = int(np.clip(ATT_Q_ROWS * i - WIN_ROWS // 2, 0, rows - ATT_BAND_ROWS))
        per_row = []
        for rq in range(ATT_Q_ROWS):
            r = ATT_Q_ROWS * i + rq
            kr = band_start + np.arange(ATT_BAND_ROWS)
            win_start = int(np.clip(r - WIN_ROWS // 2, 0, rows - WIN_ROWS))
            row_valid = tuple(bool(ok) for ok in (kr >= win_start) & (kr < win_start + WIN_ROWS))
            first_off = band_start - r + WIN_ROWS - 1
            assert -pad <= first_off <= n_off
            per_row.append((first_off + pad, row_valid))
        plan.append(tuple(per_row))

    tq, band = ATT_Q_ROWS * GRID_W, ATT_BAND_ROWS * GRID_W
    return pl.pallas_call(
        functools.partial(_table_kernel, plan=tuple(plan)),
        grid=(N_ATT_HEADS,),
        in_specs=[pl.BlockSpec((1, n_pairs, GRID_W, 2 * GRID_W), lambda h: (h, 0, 0, 0))],
        out_specs=pl.BlockSpec((3, 1, tq, band), lambda h: (0, h, 0, 0)),
        out_shape=jax.ShapeDtypeStruct((3, N_ATT_HEADS, tq, band), F32),
        name="bias_table",
    )(pairs)


def _table_kernel(pairs_ref, o_ref, *, plan):
    qc = lax.broadcasted_iota(jnp.int32, (GRID_W, 2 * GRID_W), 0)
    m = lax.broadcasted_iota(jnp.int32, (GRID_W, 2 * GRID_W), 1)
    kc = jnp.where(m < GRID_W, m, m - GRID_W)
    col_start = jnp.clip(qc - WIN_COLS // 2, 0, GRID_W - WIN_COLS)
    col_ok = (kc >= col_start) & (kc < col_start + WIN_COLS)
    first_half = m < GRID_W
    masked = jnp.full((GRID_W, 2 * GRID_W), NEG_INF, F32)
    for kind, per_row in enumerate(plan):
        for rq, (lo, row_valid) in enumerate(per_row):
            for jp in range(ATT_BAND_ROWS // 2):
                ok0, ok1 = row_valid[2 * jp], row_valid[2 * jp + 1]
                if ok0 and ok1:
                    ok = col_ok
                elif ok0:
                    ok = col_ok & first_half
                elif ok1:
                    ok = col_ok & jnp.logical_not(first_half)
                else:
                    ok = None
                tile = masked if ok is None else jnp.where(ok, pairs_ref[0, lo + 2 * jp], NEG_INF)
                o_ref[kind, 0, rq * GRID_W:(rq + 1) * GRID_W, jp * 2 * GRID_W:(jp + 1) * 2 * GRID_W] = tile


def kernel(x, c, ctx, c_ctx, w_mod, b_mod, norm_pre, norm_post, ffn1_w_gu, ffn1_w_down, ffn2_w_gu,
           ffn2_w_down, w_in, w_out, conv_w, conv_b, lru_wa, lru_ba, lru_wx, lru_bx, lru_lambda, na_rpb):
    assert w_mod.shape[0] == 1, "single-layer problem"
    b, t, d = x.shape
    rows = t // GRID_W
    assert d == D_MODEL and t % (ATT_Q_ROWS * GRID_W) == 0 and rows >= 3 * ATT_Q_ROWS
    assert b + 1 <= MOD_ROWS

    c_rows = jnp.concatenate([c, c_ctx[None], jnp.zeros((MOD_ROWS - b - 1, d), F32)], axis=0)
    mod_all = _modulation(c_rows, w_mod[0], b_mod[0])
    mod = mod_all[:b].reshape(b, N_MOD, d)
    mod_ctx = jnp.broadcast_to(mod_all[b].reshape(1, N_MOD, d), (b, N_MOD, d))

    npre, npost = norm_pre[0], norm_post[0]
    wgu1, wd1 = ffn1_w_gu[0].astype(BF16), ffn1_w_down[0].astype(BF16)
    wgu2, wd2 = ffn2_w_gu[0].astype(BF16), ffn2_w_down[0].astype(BF16)
    win, wout = w_in[0].astype(BF16), w_out[0].astype(BF16)

    q_scale = HEAD_DIM ** -0.5
    x1, xr, gr, q, k, v = _layer_in(
        x, mod, npre, npost, wgu1, wd1, win,
        proj=((0, D_REC, 1.0, F32), (C_RG, D_REC, 1.0, F32), (C_Q, D_ATT, q_scale, BF16),
              (C_K, D_ATT, 1.0, BF16), (C_V, D_ATT, 1.0, BF16)),
        write_x=True)
    xr_ctx, k_ctx, v_ctx = _layer_in(
        ctx, mod_ctx, npre, npost, wgu1, wd1, win,
        proj=((0, D_REC, 1.0, F32), (C_K, D_ATT, 1.0, BF16), (C_V, D_ATT, 1.0, BF16)),
        write_x=False)

    wg, bg, lam = _rglru_params(lru_wa[0], lru_ba[0], lru_wx[0], lru_bx[0], lru_lambda[0])
    y_rec = _rglru(xr, xr_ctx, gr, conv_w[0], conv_b[0], wg, bg, lam)
    y_att = _attention(q, k, v, k_ctx, v_ctx, _bias_tables(na_rpb[0].astype(F32), rows))
    return _layer_out(x1, y_rec, y_att, mod, npre, npost, wout, wgu2, wd2)
```

```python
import functools

import numpy as np
import jax
import jax.numpy as jnp
from jax import lax
from jax.experimental import pallas as pl
from jax.experimental.pallas import tpu as pltpu

F32 = jnp.float32
BF16 = jnp.bfloat16

D_MODEL = 1024
D_REC = 512
D_ATT = 512
HEAD_DIM = 64
N_ATT_HEADS = 8
N_REC_BLOCKS = 8
REC_BLOCK = 64
CONV_W = 4
CONV_PAD_LEFT = 2
LRU_C = 8.0
GRID_W = 64
WIN_ROWS = 8
WIN_COLS = 16
D_FF = 2816
N_MOD = 9
EPS = 1e-6
NEG_INF = -1e30
C_RG, C_Q, C_K, C_V = 512, 1024, 1536, 2048
D_IN = 2560

LANES = 128
SUBLANES = 8
VMEM_LIMIT_BYTES = 56 * 1024 * 1024

TOKEN_TILE = 512
FF_CHUNK = 256
MOD_ROWS = 16
REC_GROUP = LANES
GATE_CHUNK = 512
SCAN_BLOCK = 64
SCAN_UNROLL = 4
ATT_Q_ROWS = 4
ATT_BAND_ROWS = 12
HEAD_PAIR = LANES // HEAD_DIM
ATT_ROW_CHUNK = 32


def _sigmoid(x):
    return 0.5 * (jnp.tanh(0.5 * x) + 1.0)


def _rms(x, g):
    return x * lax.rsqrt(jnp.mean(x * x, axis=-1, keepdims=True) + EPS) * g


def _dot(a, b):
    return jnp.dot(a, b, preferred_element_type=F32)


def _resident(shape):
    return pl.BlockSpec(shape, lambda *_: (0,) * len(shape), pipeline_mode=pl.Buffered(1))


def _mod_kernel(c_ref, w_ref, b_ref, o_ref):
    c = c_ref[...]
    s = (c * _sigmoid(c)).astype(BF16)
    o_ref[...] = _dot(s, w_ref[...].astype(BF16)) + b_ref[...]


def _modulation(c_rows, w_mod, b_mod):
    n = w_mod.shape[1]
    return pl.pallas_call(
        _mod_kernel,
        grid=(n // D_MODEL,),
        in_specs=[
            pl.BlockSpec((MOD_ROWS, D_MODEL), lambda j: (0, 0)),
            pl.BlockSpec((D_MODEL, D_MODEL), lambda j: (0, j)),
            pl.BlockSpec((1, D_MODEL), lambda j: (0, j)),
        ],
        out_specs=pl.BlockSpec((MOD_ROWS, D_MODEL), lambda j: (0, j)),
        out_shape=jax.ShapeDtypeStruct((MOD_ROWS, n), F32),
        name="mod",
    )(c_rows, w_mod, b_mod.reshape(1, n))


def _swiglu(h, wgu_ref, wd_ref):
    acc = None
    for c in range(D_FF // FF_CHUNK):
        lo = c * FF_CHUNK
        g = _dot(h, wgu_ref[:, lo:lo + FF_CHUNK])
        u = _dot(h, wgu_ref[:, D_FF + lo:D_FF + lo + FF_CHUNK])
        a = (g * _sigmoid(g) * u).astype(BF16)
        d = _dot(a, wd_ref[lo:lo + FF_CHUNK, :])
        acc = d if acc is None else acc + d
    return acc


def _ffn_sublayer(x, mod_ref, npre_ref, npost_ref, idx, wgu_ref, wd_ref):
    shift = mod_ref[0, 3 * idx:3 * idx + 1, :]
    scale = mod_ref[0, 3 * idx + 1:3 * idx + 2, :]
    gate = mod_ref[0, 3 * idx + 2:3 * idx + 3, :]
    h = _rms(x, npre_ref[idx:idx + 1, :]) * (1.0 + scale) + shift
    y = _swiglu(h.astype(BF16), wgu_ref, wd_ref)
    return x + 0.5 * gate * _rms(y, npost_ref[idx:idx + 1, :])


def _layer_in_kernel(x_ref, mod_ref, npre_ref, npost_ref, wgu_ref, wd_ref, win_ref, *out_refs,
                     proj_cols, write_x):
    x1 = _ffn_sublayer(x_ref[0], mod_ref, npre_ref, npost_ref, 0, wgu_ref, wd_ref)
    outs = list(out_refs)
    if write_x:
        outs.pop(0)[0] = x1
    shift = mod_ref[0, 3:4, :]
    scale = mod_ref[0, 4:5, :]
    h = (_rms(x1, npre_ref[1:2, :]) * (1.0 + scale) + shift).astype(BF16)
    for o_ref, (lo, width, mult) in zip(outs, proj_cols):
        p = _dot(h, win_ref[:, lo:lo + width])
        if mult != 1.0:
            p = p * mult
        o_ref[0] = p.astype(o_ref.dtype)


def _layer_in(x, mod, npre, npost, wgu, wd, win, proj, write_x):
    b, t, d = x.shape
    tm = min(TOKEN_TILE, t)
    tok = lambda w: pl.BlockSpec((1, tm, w), lambda i, j: (i, j, 0))
    out_shape, out_specs = [], []
    if write_x:
        out_shape.append(jax.ShapeDtypeStruct((b, t, d), F32))
        out_specs.append(tok(d))
    for _, width, _, dtype in proj:
        out_shape.append(jax.ShapeDtypeStruct((b, t, width), dtype))
        out_specs.append(tok(width))
    return pl.pallas_call(
        functools.partial(_layer_in_kernel, proj_cols=tuple(p[:3] for p in proj), write_x=write_x),
        grid=(b, t // tm),
        in_specs=[
            tok(d),
            pl.BlockSpec((1, N_MOD, d), lambda i, j: (i, 0, 0)),
            _resident(npre.shape), _resident(npost.shape),
            _resident(wgu.shape), _resident(wd.shape), _resident(win.shape),
        ],
        out_specs=out_specs,
        out_shape=out_shape,
        compiler_params=pltpu.CompilerParams(
            dimension_semantics=("parallel", "parallel"), vmem_limit_bytes=VMEM_LIMIT_BYTES),
        name="layer_in",
    )(x, mod, npre, npost, wgu, wd, win)


def _layer_out_kernel(x_ref, yr_ref, ya_ref, mod_ref, npre_ref, npost_ref, wout_ref, wgu_ref, wd_ref,
                      o_ref):
    y = _dot(yr_ref[0], wout_ref[:D_REC, :]) + _dot(ya_ref[0], wout_ref[D_REC:, :])
    x2 = x_ref[0] + mod_ref[0, 5:6, :] * _rms(y, npost_ref[1:2, :])
    o_ref[0] = _ffn_sublayer(x2, mod_ref, npre_ref, npost_ref, 2, wgu_ref, wd_ref)


def _layer_out(x1, y_rec, y_att, mod, npre, npost, wout, wgu, wd):
    b, t, d = x1.shape
    tm = min(TOKEN_TILE, t)
    tok = lambda w: pl.BlockSpec((1, tm, w), lambda i, j: (i, j, 0))
    return pl.pallas_call(
        _layer_out_kernel,
        grid=(b, t // tm),
        in_specs=[
            tok(d), tok(D_REC), tok(D_ATT),
            pl.BlockSpec((1, N_MOD, d), lambda i, j: (i, 0, 0)),
            _resident(npre.shape), _resident(npost.shape),
            _resident(wout.shape), _resident(wgu.shape), _resident(wd.shape),
        ],
        out_specs=tok(d),
        out_shape=jax.ShapeDtypeStruct((b, t, d), F32),
        compiler_params=pltpu.CompilerParams(
            dimension_semantics=("parallel", "parallel"), vmem_limit_bytes=VMEM_LIMIT_BYTES),
        name="layer_out",
    )(x1, y_rec, y_att, mod, npre, npost, wout, wgu, wd)


def _sublane_scan(a, u, row, reverse):
    for s in (1, 2, 4):
        if reverse:
            keep = row < SUBLANES - s
            shift = SUBLANES - s
        else:
            keep = row >= s
            shift = s
        a_sh = jnp.where(keep, pltpu.roll(a, shift, 0), 1.0)
        u_sh = jnp.where(keep, pltpu.roll(u, shift, 0), 0.0)
        u = a * u_sh + u
        a = a * a_sh
    return a, u


def _scan_block(a_ref, u_ref, h_ref, t0, carry, row, reverse):
    steps = range(SUBLANES - 1, -1, -1) if reverse else range(SUBLANES)
    h = p = None
    hs, ps = [None] * SUBLANES, [None] * SUBLANES
    for r in steps:
        a = a_ref[pl.ds(t0 + r, SUBLANES, stride=SUBLANES), :]
        u = u_ref[pl.ds(t0 + r, SUBLANES, stride=SUBLANES), :]
        h, p = (u, a) if h is None else (a * h + u, a * p)
        hs[r], ps[r] = h, p
    pp, hh = _sublane_scan(p, h, row, reverse)
    after = pp * carry + hh
    if reverse:
        before = jnp.where(row < SUBLANES - 1, pltpu.roll(after, SUBLANES - 1, 0), carry)
    else:
        before = jnp.where(row >= 1, pltpu.roll(after, 1, 0), carry)
    for r in range(SUBLANES):
        h_ref[pl.ds(t0 + r, SUBLANES, stride=SUBLANES), :] = hs[r] + ps[r] * before
    last = 0 if reverse else SUBLANES - 1
    return pp[last:last + 1, :] * carry + hh[last:last + 1, :]


def _rglru_kernel(xr_ref, xc_ref, gr_ref, cw_ref, cb_ref, wg_ref, bg_ref, lam_ref, o_ref,
                  pad_s, conv_s, af_s, uf_s, ab_s, ub_s, hf_s, hb_s):
    cg = REC_GROUP
    lam = lam_ref[0]
    log_sig_lam = jnp.minimum(lam, 0.0) - jnp.log(1.0 + jnp.exp(-jnp.abs(lam)))
    neg_half_c_lsl = (-0.5 * LRU_C) * log_sig_lam
    row = lax.broadcasted_iota(jnp.int32, (SUBLANES, cg), 0)

    def conv_gates(src_ref, n):
        halo = SUBLANES
        pad_s[0:halo, :] = jnp.zeros((halo, cg), F32)
        pad_s[halo:halo + n, :] = src_ref[0]
        pad_s[halo + n:2 * halo + n, :] = jnp.zeros((halo, cg), F32)
        y = cb_ref[...]
        for k in range(CONV_W):
            off = halo + k - CONV_PAD_LEFT
            y = y + pad_s[off:off + n, :] * cw_ref[k:k + 1, :]
        conv_s[0:n, :] = y

        chunk = min(GATE_CHUNK, n)

        def gate_body(ci, _):
            r0 = pl.multiple_of(ci * chunk, chunk)
            xv = conv_s[pl.ds(r0, chunk), :]
            g = _dot(xv.astype(BF16), wg_ref[0]) + bg_ref[0]
            x_half = 0.5 * xv
            for d, (a_s, u_s) in enumerate(((af_s, uf_s), (ab_s, ub_s))):
                ta = jnp.tanh(g[:, (2 * d) * cg:(2 * d + 1) * cg])
                tx = jnp.tanh(g[:, (2 * d + 1) * cg:(2 * d + 2) * cg])
                neg_log_a = neg_half_c_lsl[:, d * cg:(d + 1) * cg] * (ta + 1.0)
                a = jnp.exp(-neg_log_a)
                a_s[pl.ds(r0, chunk), :] = a
                one_minus_a2 = jnp.tanh(neg_log_a) * (a * a + 1.0)
                root = jnp.where(one_minus_a2 > 0.0, one_minus_a2 * lax.rsqrt(one_minus_a2), 0.0)
                u_s[pl.ds(r0, chunk), :] = root * (tx + 1.0) * x_half
            return 0

        lax.fori_loop(0, n // chunk, gate_body, 0)

    def scan(n, hf0, hb0):
        blocks = n // SCAN_BLOCK

        def body(k, carry):
            hf, hb = carry
            tf = pl.multiple_of(k * SCAN_BLOCK, SCAN_BLOCK)
            tb = pl.multiple_of((blocks - 1 - k) * SCAN_BLOCK, SCAN_BLOCK)
            hf = _scan_block(af_s, uf_s, hf_s, tf, hf, row, False)
            hb = _scan_block(ab_s, ub_s, hb_s, tb, hb, row, True)
            return hf, hb

        return lax.fori_loop(0, blocks, body, (hf0, hb0), unroll=max(1, min(SCAN_UNROLL, blocks // 2)))

    n_ctx = xc_ref.shape[1]
    n_lat = xr_ref.shape[1]
    zero = jnp.zeros((1, cg), F32)
    conv_gates(xc_ref, n_ctx)
    hf, hb = scan(n_ctx, zero, zero)
    conv_gates(xr_ref, n_lat)
    scan(n_lat, hf, hb)
    y = hf_s[0:n_lat, :] + hb_s[0:n_lat, :]
    o_ref[0] = (y * jax.nn.gelu(gr_ref[0])).astype(o_ref.dtype)


def _rglru(xr, xr_ctx, gr, conv_w, conv_b, wg, bg, lam):
    b, t, _ = xr.shape
    n_ctx = xr_ctx.shape[1]
    cg = REC_GROUP
    seq = lambda n: pl.BlockSpec((1, n, cg), lambda i, j: (i, 0, j))
    grp = lambda shape: pl.BlockSpec((1,) + shape, lambda i, j: (j, 0, 0))
    return pl.pallas_call(
        _rglru_kernel,
        grid=(b, D_REC // cg),
        in_specs=[
            seq(t), seq(n_ctx), seq(t),
            pl.BlockSpec((CONV_W, cg), lambda i, j: (0, j)),
            pl.BlockSpec((1, cg), lambda i, j: (0, j)),
            grp((cg, 4 * cg)), grp((1, 4 * cg)), grp((1, 2 * cg)),
        ],
        out_specs=seq(t),
        out_shape=jax.ShapeDtypeStruct((b, t, D_REC), BF16),
        scratch_shapes=[pltpu.VMEM((t + 2 * SUBLANES, cg), F32)] + [pltpu.VMEM((t, cg), F32)] * 7,
        compiler_params=pltpu.CompilerParams(
            dimension_semantics=("parallel", "parallel"), vmem_limit_bytes=VMEM_LIMIT_BYTES),
        name="rglru",
    )(xr, xr_ctx, gr, conv_w, conv_b.reshape(1, D_REC), wg, bg, lam)


def _rglru_params(lru_wa, lru_ba, lru_wx, lru_bx, lru_lambda):
    cg = REC_GROUP
    n_grp = D_REC // cg
    per = cg // REC_BLOCK

    def block_diag(w):
        w = w.reshape(n_grp, per, REC_BLOCK, REC_BLOCK)
        eye = jnp.eye(per, dtype=w.dtype)
        return jnp.einsum("gpcd,pq->gpcqd", w, eye).reshape(n_grp, cg, cg)

    wg = (0.5 * jnp.concatenate([block_diag(lru_wa[0]), block_diag(lru_wx[0]),
                                 block_diag(lru_wa[1]), block_diag(lru_wx[1])], axis=-1)).astype(BF16)
    vec = lambda v: v.reshape(n_grp, 1, cg)
    bg = 0.5 * jnp.concatenate([vec(lru_ba[0]), vec(lru_bx[0]), vec(lru_ba[1]), vec(lru_bx[1])], axis=-1)
    lam = jnp.concatenate([vec(lru_lambda[0]), vec(lru_lambda[1])], axis=-1)
    return wg, bg, lam


def _attn_kernel(q_ref, k_ref, v_ref, kc_ref, vc_ref, tbl_ref, o_ref, s_buf, p_buf, *, rows):
    i = pl.program_id(1)
    n_blk = pl.num_programs(1)
    start = jnp.clip(ATT_Q_ROWS * i - WIN_ROWS // 2, 0, rows - ATT_BAND_ROWS)
    kind = jnp.where(i == 0, 0, jnp.where(i == n_blk - 1, 2, 1))
    off = pl.multiple_of(start * GRID_W, GRID_W)
    band = ATT_BAND_ROWS * GRID_W
    tq = ATT_Q_ROWS * GRID_W
    lane = lax.broadcasted_iota(jnp.int32, (tq, LANES), 1)
    contract_last = (((1,), (1,)), ((), ()))

    def head_cols(head):
        pair = head // HEAD_PAIR
        return slice(pair * LANES, (pair + 1) * LANES)

    def scores(head):
        cols, hh, slot = head_cols(head), head % HEAD_PAIR, head % 2
        q = q_ref[0, :, cols]
        mine = (lane >= hh * HEAD_DIM) & (lane < (hh + 1) * HEAD_DIM)
        qm = jnp.where(mine, q, jnp.zeros_like(q))
        s_loc = lax.dot_general(qm, k_ref[0, pl.ds(off, band), cols], contract_last,
                                preferred_element_type=F32)
        s_buf[slot, :, :band] = s_loc
        s_buf[slot, :, band:] = lax.dot_general(qm, kc_ref[0, :, cols], contract_last,
                                                preferred_element_type=F32)

    def softmax(head):
        slot = head % 2
        sums = []
        for r0 in range(0, tq, ATT_ROW_CHUNK):
            s = jnp.concatenate(
                [s_buf[slot, r0:r0 + ATT_ROW_CHUNK, :band] + tbl_ref[kind, head, r0:r0 + ATT_ROW_CHUNK, :],
                 s_buf[slot, r0:r0 + ATT_ROW_CHUNK, band:]], axis=-1)
            e =jnp.exp(s - jnp.max(s, axis=-1, keepdims=True))
            sums.append(jnp.sum(e, axis=-1, keepdims=True))
            p_buf[slot, r0:r0 + ATT_ROW_CHUNK, :] = e.astype(BF16)
        return jnp.concatenate(sums, axis=0)

    def values(head, denom):
        cols, slot = head_cols(head), head % 2
        o = (_dot(p_buf[slot, :, :band], v_ref[0, pl.ds(off, band), cols])
             + _dot(p_buf[slot, :, band:], vc_ref[0, :, cols]))
        return o / denom

    scores(0)
    outs = []
    for head in range(N_ATT_HEADS):
        if head + 1 < N_ATT_HEADS:
            scores(head + 1)
        outs.append(values(head, softmax(head)))
        if head % HEAD_PAIR == HEAD_PAIR - 1:
            o_ref[0, :, head_cols(head)] = jnp.where(lane < HEAD_DIM, outs[-2], outs[-1]).astype(o_ref.dtype)


def _attention(q, k, v, k_ctx, v_ctx, tbl):
    b, t, _ = q.shape
    n_ctx = k_ctx.shape[1]
    rows = t // GRID_W
    tq = ATT_Q_ROWS * GRID_W
    n_keys = ATT_BAND_ROWS * GRID_W + n_ctx
    full = lambda n: pl.BlockSpec((1, n, D_ATT), lambda bi, i: (bi, 0, 0))
    blk = pl.BlockSpec((1, tq, D_ATT), lambda bi, i: (bi, i, 0))
    return pl.pallas_call(
        functools.partial(_attn_kernel, rows=rows),
        grid=(b, rows // ATT_Q_ROWS),
        in_specs=[blk, full(t), full(t), full(n_ctx), full(n_ctx), _resident(tbl.shape)],
        out_specs=blk,
        out_shape=jax.ShapeDtypeStruct((b, t, D_ATT), BF16),
        scratch_shapes=[pltpu.VMEM((2, tq, n_keys), F32), pltpu.VMEM((2, tq, n_keys), BF16)],
        compiler_params=pltpu.CompilerParams(
            dimension_semantics=("parallel", "arbitrary"), vmem_limit_bytes=VMEM_LIMIT_BYTES),
        name="attn",
    )(q, k, v, k_ctx, v_ctx, tbl)


def _bias_tables(rpb, rows):
    n_blk = rows // ATT_Q_ROWS
    n_off = 2 * WIN_ROWS - 1
    pad = ATT_BAND_ROWS
    n_pairs = n_off + 2 * pad
    col = np.arange(GRID_W)
    col_off = np.clip(col[None, :] - col[:, None] + WIN_COLS - 1, 0, 2 * WIN_COLS - 2)
    n_col_off = 2 * WIN_COLS - 1
    rp = jnp.pad(rpb, ((0, 0), (pad, pad + 1), (0, 0)))
    z = jnp.concatenate([rp[:, :n_pairs], rp[:, 1:n_pairs + 1]], axis=-1)
    m = np.arange(2 * GRID_W)
    f = np.arange(2 * n_col_off)
    select = ((f[:, None, None] // n_col_off == m[None, None, :] // GRID_W)
              & (f[:, None, None] % n_col_off == col_off[:, m % GRID_W][None])).astype(np.float32)
    pairs = jnp.einsum("hRf,fqm->hRqm", z, select, precision=lax.Precision.HIGHEST)

    plan = []
    for i in (0, 1, n_blk - 1):
        band_start = int(np.clip(ATT_Q_ROWS * i - WIN_ROWS // 2, 0, rows - ATT_BAND_ROWS))
        per_row = []
        for rq in range(ATT_Q_ROWS):
            r = ATT_Q_ROWS * i + rq
            kr = band_start + np.arange(ATT_BAND_ROWS)
            win_start = int(np.clip(r - WIN_ROWS // 2, 0, rows - WIN_ROWS))
            row_valid = tuple(bool(ok) for ok in (kr >= win_start) & (kr < win_start + WIN_ROWS))
            first_off = band_start - r + WIN_ROWS - 1
            assert -pad <= first_off <= n_off
            per_row.append((first_off + pad, row_valid))
        plan.append(tuple(per_row))

    tq, band = ATT_Q_ROWS * GRID_W, ATT_BAND_ROWS * GRID_W
    return pl.pallas_call(
        functools.partial(_table_kernel, plan=tuple(plan)),
        grid=(N_ATT_HEADS,),
        in_specs=[pl.BlockSpec((1, n_pairs, GRID_W, 2 * GRID_W), lambda h: (h, 0, 0, 0))],
        out_specs=pl.BlockSpec((3, 1, tq, band), lambda h: (0, h, 0, 0)),
        out_shape=jax.ShapeDtypeStruct((3, N_ATT_HEADS, tq, band), F32),
        name="bias_table",
    )(pairs)


def _table_kernel(pairs_ref, o_ref, *, plan):
    qc = lax.broadcasted_iota(jnp.int32, (GRID_W, 2 * GRID_W), 0)
    m = lax.broadcasted_iota(jnp.int32, (GRID_W, 2 * GRID_W), 1)
    kc = jnp.where(m < GRID_W, m, m - GRID_W)
    col_start = jnp.clip(qc - WIN_COLS // 2, 0, GRID_W - WIN_COLS)
    col_ok = (kc >= col_start) & (kc < col_start + WIN_COLS)
    first_half = m < GRID_W
    masked = jnp.full((GRID_W, 2 * GRID_W), NEG_INF, F32)
    for kind, per_row in enumerate(plan):
        for rq, (lo, row_valid) in enumerate(per_row):
            for jp in range(ATT_BAND_ROWS // 2):
                ok0, ok1 = row_valid[2 * jp], row_valid[2 * jp + 1]
                if ok0 and ok1:
                    ok = col_ok
                elif ok0:
                    ok = col_ok & first_half
                elif ok1:
                    ok = col_ok & jnp.logical_not(first_half)
                else:
                    ok = None
                tile = masked if ok is None else jnp.where(ok, pairs_ref[0, lo + 2 * jp], NEG_INF)
                o_ref[kind, 0, rq * GRID_W:(rq + 1) * GRID_W, jp * 2 * GRID_W:(jp + 1) * 2 * GRID_W] = tile


def kernel(x, c, ctx, c_ctx, w_mod, b_mod, norm_pre, norm_post, ffn1_w_gu, ffn1_w_down, ffn2_w_gu,
           ffn2_w_down, w_in, w_out, conv_w, conv_b, lru_wa, lru_ba, lru_wx, lru_bx, lru_lambda, na_rpb):
    assert w_mod.shape[0] == 1, "single-layer problem"
    b, t, d = x.shape
    rows = t // GRID_W
    assert d == D_MODEL and t % (ATT_Q_ROWS * GRID_W) == 0 and rows >= 3 * ATT_Q_ROWS
    assert b + 1 <= MOD_ROWS

    c_rows = jnp.concatenate([c, c_ctx[None], jnp.zeros((MOD_ROWS - b - 1, d), F32)], axis=0)
    mod_all = _modulation(c_rows, w_mod[0], b_mod[0])
    mod = mod_all[:b].reshape(b, N_MOD, d)
    mod_ctx = jnp.broadcast_to(mod_all[b].reshape(1, N_MOD, d), (b, N_MOD, d))

    npre, npost = norm_pre[0], norm_post[0]
    wgu1, wd1 = ffn1_w_gu[0].astype(BF16), ffn1_w_down[0].astype(BF16)
    wgu2, wd2 = ffn2_w_gu[0].astype(BF16), ffn2_w_down[0].astype(BF16)
    win, wout = w_in[0].astype(BF16), w_out[0].astype(BF16)

    q_scale = HEAD_DIM ** -0.5
    x1, xr, gr, q, k, v = _layer_in(
        x, mod, npre, npost, wgu1, wd1, win,
        proj=((0, D_REC, 1.0, F32), (C_RG, D_REC, 1.0, F32), (C_Q, D_ATT, q_scale, BF16),
              (C_K, D_ATT, 1.0, BF16), (C_V, D_ATT, 1.0, BF16)),
        write_x=True)
    xr_ctx, k_ctx, v_ctx = _layer_in(
        ctx, mod_ctx, npre, npost, wgu1, wd1, win,
        proj=((0, D_REC, 1.0, F32), (C_K, D_ATT, 1.0, BF16), (C_V, D_ATT, 1.0, BF16)),
        write_x=False)

    wg, bg, lam = _rglru_params(lru_wa[0], lru_ba[0], lru_wx[0], lru_bx[0], lru_lambda[0])
    y_rec = _rglru(xr, xr_ctx, gr, conv_w[0], conv_b[0], wg, bg, lam)
    y_att = _attention(q, k, v, k_ctx, v_ctx, _bias_tables(na_rpb[0].astype(F32), rows))
    return _layer_out(x1, y_rec, y_att, mod, npre, npost, wout, wgu2, wd2)
```

```python
import functools

import numpy as np
import jax
import jax.numpy as jnp
from jax import lax
from jax.experimental import pallas as pl
from jax.experimental.pallas import tpu as pltpu

F32 = jnp.float32
BF16 = jnp.bfloat16

D_MODEL = 1024
D_REC = 512
D_ATT = 512
HEAD_DIM = 64
N_ATT_HEADS = 8
N_REC_BLOCKS = 8
REC_BLOCK = 64
CONV_W = 4
CONV_PAD_LEFT = 2
LRU_C = 8.0
GRID_W = 64
WIN_ROWS = 8
WIN_COLS = 16
D_FF = 2816
N_MOD = 9
EPS = 1e-6
NEG_INF = -1e30
LOG2_E = 1.4426950408889634
C_RG, C_Q, C_K, C_V = 512, 1024, 1536, 2048
D_IN = 2560

LANES = 128
SUBLANES = 8
VMEM_LIMIT_BYTES = 56 * 1024 * 1024

TOKEN_TILE = 512
FF_CHUNK = 256
MOD_ROWS = 16
REC_GROUP = LANES
GATE_CHUNK = 2048
SCAN_BLOCK = 64
SCAN_UNROLL = 4
ATT_Q_ROWS = 4
ATT_BAND_ROWS = 12
HEAD_PAIR = LANES // HEAD_DIM
ATT_ROW_CHUNK = 32


def _sigmoid(x):
    return 0.5 * (jnp.tanh(0.5 * x) + 1.0)


def _rms(x, g):
    return x * lax.rsqrt(jnp.mean(x * x, axis=-1, keepdims=True) + EPS) * g


def _dot(a, b):
    return jnp.dot(a, b, preferred_element_type=F32)


def _resident(shape):
    return pl.BlockSpec(shape, lambda *_: (0,) * len(shape), pipeline_mode=pl.Buffered(1))


def _mod_kernel(c_ref, w_ref, b_ref, o_ref):
    c = c_ref[...]
    s = (c * _sigmoid(c)).astype(BF16)
    o_ref[...] = _dot(s, w_ref[...].astype(BF16)) + b_ref[...]


def _modulation(c_rows, w_mod, b_mod):
    n = w_mod.shape[1]
    return pl.pallas_call(
        _mod_kernel,
        grid=(n // D_MODEL,),
        in_specs=[
            pl.BlockSpec((MOD_ROWS, D_MODEL), lambda j: (0, 0)),
            pl.BlockSpec((D_MODEL, D_MODEL), lambda j: (0, j)),
            pl.BlockSpec((1, D_MODEL), lambda j: (0, j)),
        ],
        out_specs=pl.BlockSpec((MOD_ROWS, D_MODEL), lambda j: (0, j)),
        out_shape=jax.ShapeDtypeStruct((MOD_ROWS, n), F32),
        name="mod",
    )(c_rows, w_mod, b_mod.reshape(1, n))


def _swiglu(h, wgu_ref, wd_ref):
    acc = None
    for c in range(D_FF // FF_CHUNK):
        lo = c * FF_CHUNK
        g = _dot(h, wgu_ref[:, lo:lo + FF_CHUNK])
        u = _dot(h, wgu_ref[:, D_FF + lo:D_FF + lo + FF_CHUNK])
        a = (g * _sigmoid(g) * u).astype(BF16)
        d = _dot(a, wd_ref[lo:lo + FF_CHUNK, :])
        acc = d if acc is None else acc + d
    return acc


def _ffn_sublayer(x, mod_ref, npre_ref, npost_ref, idx, wgu_ref, wd_ref):
    shift = mod_ref[0, 3 * idx:3 * idx + 1, :]
    scale = mod_ref[0, 3 * idx + 1:3 * idx + 2, :]
    gate = mod_ref[0, 3 * idx + 2:3 * idx + 3, :]
    h = _rms(x, npre_ref[idx:idx + 1, :]) * (1.0 + scale) + shift
    y = _swiglu(h.astype(BF16), wgu_ref, wd_ref)
    return x + 0.5 * gate * _rms(y, npost_ref[idx:idx + 1, :])


def _layer_in_kernel(x_ref, mod_ref, npre_ref, npost_ref, wgu_ref, wd_ref, win_ref, *out_refs,
                     proj_cols, write_x):
    x1 = _ffn_sublayer(x_ref[0], mod_ref, npre_ref, npost_ref, 0, wgu_ref, wd_ref)
    outs = list(out_refs)
    if write_x:
        outs.pop(0)[0] = x1
    shift = mod_ref[0, 3:4, :]
    scale = mod_ref[0, 4:5, :]
    h = (_rms(x1, npre_ref[1:2, :]) * (1.0 + scale) + shift).astype(BF16)
    for o_ref, (lo, width, mult) in zip(outs, proj_cols):
        p = _dot(h, win_ref[:, lo:lo + width])
        if mult != 1.0:
            p = p * mult
        o_ref[0] = p.astype(o_ref.dtype)


def _layer_in(x, mod, npre, npost, wgu, wd, win, proj, write_x):
    b, t, d = x.shape
    tm = min(TOKEN_TILE, t)
    tok = lambda w: pl.BlockSpec((1, tm, w), lambda i, j: (i, j, 0))
    out_shape, out_specs = [], []
    if write_x:
        out_shape.append(jax.ShapeDtypeStruct((b, t, d), F32))
        out_specs.append(tok(d))
    for _, width, _, dtype in proj:
        out_shape.append(jax.ShapeDtypeStruct((b, t, width), dtype))
        out_specs.append(tok(width))
    return pl.pallas_call(
        functools.partial(_layer_in_kernel, proj_cols=tuple(p[:3] for p in proj), write_x=write_x),
        grid=(b, t // tm),
        in_specs=[
            tok(d),
            pl.BlockSpec((1, N_MOD, d), lambda i, j: (i, 0, 0)),
            _resident(npre.shape), _resident(npost.shape),
            _resident(wgu.shape), _resident(wd.shape), _resident(win.shape),
        ],
        out_specs=out_specs,
        out_shape=out_shape,
        compiler_params=pltpu.CompilerParams(
            dimension_semantics=("parallel", "parallel"), vmem_limit_bytes=VMEM_LIMIT_BYTES),
        name="layer_in",
    )(x, mod, npre, npost, wgu, wd, win)


def _layer_out_kernel(x_ref, yr_ref, ya_ref, mod_ref, npre_ref, npost_ref, wout_ref, wgu_ref, wd_ref,
                      o_ref):
    y = _dot(yr_ref[0], wout_ref[:D_REC, :]) + _dot(ya_ref[0], wout_ref[D_REC:, :])
    x2 = x_ref[0] + mod_ref[0, 5:6, :] * _rms(y, npost_ref[1:2, :])
    o_ref[0] = _ffn_sublayer(x2, mod_ref, npre_ref, npost_ref, 2, wgu_ref, wd_ref)


def _layer_out(x1, y_rec, y_att, mod, npre, npost, wout, wgu, wd):
    b, t, d = x1.shape
    tm = min(TOKEN_TILE, t)
    tok = lambda w: pl.BlockSpec((1, tm, w), lambda i, j: (i, j, 0))
    return pl.pallas_call(
        _layer_out_kernel,
        grid=(b, t // tm),
        in_specs=[
            tok(d), tok(D_REC), tok(D_ATT),
            pl.BlockSpec((1, N_MOD, d), lambda i, j: (i, 0, 0)),
            _resident(npre.shape), _resident(npost.shape),
            _resident(wout.shape), _resident(wgu.shape), _resident(wd.shape),
        ],
        out_specs=tok(d),
        out_shape=jax.ShapeDtypeStruct((b, t, d), F32),
        compiler_params=pltpu.CompilerParams(
            dimension_semantics=("parallel", "parallel"), vmem_limit_bytes=VMEM_LIMIT_BYTES),
        name="layer_out",
    )(x1, y_rec, y_att, mod, npre, npost, wout, wgu, wd)


def _sublane_scan(a, u, row, reverse):
    for s in (1, 2, 4):
        if reverse:
            keep = row < SUBLANES - s
            shift = SUBLANES - s
        else:
            keep = row >= s
            shift = s
        a_sh = jnp.where(keep, pltpu.roll(a, shift, 0), 1.0)
        u_sh = jnp.where(keep, pltpu.roll(u, shift, 0), 0.0)
        u = a * u_sh + u
        a = a * a_sh
    return a, u


def _scan_block(a_ref, u_ref, h_ref, t0, carry, row, reverse):
    steps = range(SUBLANES - 1, -1, -1) if reverse else range(SUBLANES)
    h = p = None
    hs, ps = [None] * SUBLANES, [None] * SUBLANES
    for r in steps:
        a = a_ref[pl.ds(t0 + r, SUBLANES, stride=SUBLANES), :]
        u = u_ref[pl.ds(t0 + r, SUBLANES, stride=SUBLANES), :]
        h, p = (u, a) if h is None else (a * h + u, a * p)
        hs[r], ps[r] = h, p
    pp, hh = _sublane_scan(p, h, row, reverse)
    after = pp * carry + hh
    if reverse:
        before = jnp.where(row < SUBLANES - 1, pltpu.roll(after, SUBLANES - 1, 0), carry)
    else:
        before = jnp.where(row >= 1, pltpu.roll(after, 1, 0), carry)
    for r in range(SUBLANES):
        h_ref[pl.ds(t0 + r, SUBLANES, stride=SUBLANES), :] = hs[r] + ps[r] * before
    last = 0 if reverse else SUBLANES - 1
    return pp[last:last + 1, :] * carry + hh[last:last + 1, :]


def _rglru_kernel(xr_ref, xc_ref, gr_ref, cw_ref, cb_ref, wg_ref, bg_ref, lam_ref, o_ref,
                  pad_s, conv_s, af_s, uf_s, ab_s, ub_s, hf_s, hb_s):
    cg = REC_GROUP
    lam = lam_ref[0]
    log_sig_lam = jnp.minimum(lam, 0.0) - jnp.log(1.0 + jnp.exp(-jnp.abs(lam)))
    neg_half_c_lsl = (-0.5 * LRU_C) * log_sig_lam
    row = lax.broadcasted_iota(jnp.int32, (SUBLANES, cg), 0)

    def conv_gates(src_ref, n):
        halo = SUBLANES
        pad_s[0:halo, :] = jnp.zeros((halo, cg), F32)
        pad_s[halo:halo + n, :] = src_ref[0]
        pad_s[halo + n:2 * halo + n, :] = jnp.zeros((halo, cg), F32)
        y = cb_ref[...]
        for k in range(CONV_W):
            off = halo + k - CONV_PAD_LEFT
            y = y + pad_s[off:off + n, :] * cw_ref[k:k + 1, :]
        conv_s[0:n, :] = y

        chunk = min(GATE_CHUNK, n)

        def gate_body(ci, _):
            r0 = pl.multiple_of(ci * chunk, chunk)
            xv = conv_s[pl.ds(r0, chunk), :]
            g = _dot(xv.astype(BF16), wg_ref[0]) + bg_ref[0]
            x_half = 0.5 * xv
            for d, (a_s, u_s) in enumerate(((af_s, uf_s), (ab_s, ub_s))):
                ta = jnp.tanh(g[:, (2 * d) * cg:(2 * d + 1) * cg])
                tx = jnp.tanh(g[:, (2 * d + 1) * cg:(2 * d + 2) * cg])
                neg_log_a = neg_half_c_lsl[:, d * cg:(d + 1) * cg] * (ta + 1.0)
                a = jnp.exp2(neg_log_a * (-LOG2_E))
                a_s[pl.ds(r0, chunk), :] = a
                one_minus_a2 = jnp.tanh(neg_log_a) * (a * a + 1.0)
                root = jnp.where(one_minus_a2 > 0.0, one_minus_a2 * lax.rsqrt(one_minus_a2), 0.0)
                u_s[pl.ds(r0, chunk), :] = root * (tx + 1.0) * x_half
            return 0

        lax.fori_loop(0, n // chunk, gate_body, 0)

    def scan(n, hf0, hb0):
        blocks = n // SCAN_BLOCK

        def body(k, carry):
            hf, hb = carry
            tf = pl.multiple_of(k * SCAN_BLOCK, SCAN_BLOCK)
            tb = pl.multiple_of((blocks - 1 - k) * SCAN_BLOCK, SCAN_BLOCK)
            hf = _scan_block(af_s, uf_s, hf_s, tf, hf, row, False)
            hb = _scan_block(ab_s, ub_s, hb_s, tb, hb, row, True)
            return hf, hb

        return lax.fori_loop(0, blocks, body, (hf0, hb0), unroll=max(1, min(SCAN_UNROLL, blocks // 2)))

    n_ctx = xc_ref.shape[1]
    n_lat = xr_ref.shape[1]
    zero = jnp.zeros((1, cg), F32)
    conv_gates(xc_ref, n_ctx)
    hf, hb = scan(n_ctx, zero, zero)
    conv_gates(xr_ref, n_lat)
    scan(n_lat, hf, hb)
    y = hf_s[0:n_lat, :] + hb_s[0:n_lat, :]
    o_ref[0] = (y * jax.nn.gelu(gr_ref[0])).astype(o_ref.dtype)


def _rglru(xr, xr_ctx, gr, conv_w, conv_b, wg, bg, lam):
    b, t, _ = xr.shape
    n_ctx = xr_ctx.shape[1]
    cg = REC_GROUP
    seq = lambda n: pl.BlockSpec((1, n, cg), lambda i, j: (i, 0, j))
    grp = lambda shape: pl.BlockSpec((1,) + shape, lambda i, j: (j, 0, 0))
    return pl.pallas_call(
        _rglru_kernel,
        grid=(b, D_REC // cg),
        in_specs=[
            seq(t), seq(n_ctx), seq(t),
            pl.BlockSpec((CONV_W, cg), lambda i, j: (0, j)),
            pl.BlockSpec((1, cg), lambda i, j: (0, j)),
            grp((cg, 4 * cg)), grp((1, 4 * cg)), grp((1, 2 * cg)),
        ],
        out_specs=seq(t),
        out_shape=jax.ShapeDtypeStruct((b, t, D_REC), BF16),
        scratch_shapes=[pltpu.VMEM((t + 2 * SUBLANES, cg), F32)] + [pltpu.VMEM((t, cg), F32)] * 7,
        compiler_params=pltpu.CompilerParams(
            dimension_semantics=("parallel", "parallel"), vmem_limit_bytes=VMEM_LIMIT_BYTES),
        name="rglru",
    )(xr, xr_ctx, gr, conv_w, conv_b.reshape(1, D_REC), wg, bg, lam)


def _rglru_params(lru_wa, lru_ba, lru_wx, lru_bx, lru_lambda):
    cg = REC_GROUP
    n_grp = D_REC // cg
    per = cg // REC_BLOCK

    def block_diag(w):
        w = w.reshape(n_grp, per, REC_BLOCK, REC_BLOCK)
        eye = jnp.eye(per, dtype=w.dtype)
        return jnp.einsum("gpcd,pq->gpcqd", w, eye).reshape(n_grp, cg, cg)

    wg = (0.5 * jnp.concatenate([block_diag(lru_wa[0]), block_diag(lru_wx[0]),
                                 block_diag(lru_wa[1]), block_diag(lru_wx[1])], axis=-1)).astype(BF16)
    vec = lambda v: v.reshape(n_grp, 1, cg)
    bg = 0.5 * jnp.concatenate([vec(lru_ba[0]), vec(lru_bx[0]), vec(lru_ba[1]), vec(lru_bx[1])], axis=-1)
    lam = jnp.concatenate([vec(lru_lambda[0]), vec(lru_lambda[1])], axis=-1)
    return wg, bg, lam


def _attn_kernel(q_ref, k_ref, v_ref, kc_ref, vc_ref, tbl_ref, o_ref, s_buf, p_buf, *, rows):
    i = pl.program_id(1)
    n_blk = pl.num_programs(1)
    start = jnp.clip(ATT_Q_ROWS * i - WIN_ROWS // 2, 0, rows - ATT_BAND_ROWS)
    kind = jnp.where(i == 0, 0, jnp.where(i == n_blk - 1, 2, 1))
    off = pl.multiple_of(start * GRID_W, GRID_W)
    band = ATT_BAND_ROWS * GRID_W
    tq = ATT_Q_ROWS * GRID_W
    lane = lax.broadcasted_iota(jnp.int32, (tq, LANES), 1)
    contract_last = (((1,), (1,)), ((), ()))

    def head_cols(head):
        pair = head // HEAD_PAIR
        return slice(pair * LANES, (pair + 1) * LANES)

    def scores(head):
        cols, hh, slot = head_cols(head), head % HEAD_PAIR, head % 2
        q = q_ref[0, :, cols]
        mine = (lane >= hh * HEAD_DIM) & (lane < (hh + 1) * HEAD_DIM)
        qm = jnp.where(mine, q, jnp.zeros_like(q))
        s_loc = lax.dot_general(qm, k_ref[0, pl.ds(off, band), cols], contract_last,
                                preferred_element_type=F32)
        s_buf[slot, :, :band] = s_loc
        s_buf[slot, :, band:] = lax.dot_general(qm, kc_ref[0, :, cols], contract_last,
                                                preferred_element_type=F32)

    def softmax(head):
        slot = head % 2
        sums = []
        for r0 in range(0, tq, ATT_ROW_CHUNK):
            s = jnp.concatenate(
                [s_buf[slot, r0:r0 + ATT_ROW_CHUNK, :band] + tbl_ref[kind, head, r0:r0 + ATT_ROW_CHUNK, :],
                 s_buf[slot, r0:r0 + ATT_ROW_CHUNK, band:]], axis=-1)
            e =jnp.exp(s - jnp.max(s, axis=-1, keepdims=True))
            sums.append(jnp.sum(e, axis=-1, keepdims=True))
            p_buf[slot, r0:r0 + ATT_ROW_CHUNK, :] = e.astype(BF16)
        return jnp.concatenate(sums, axis=0)

    def values(head, denom):
        cols, slot = head_cols(head), head % 2
        o = (_dot(p_buf[slot, :, :band], v_ref[0, pl.ds(off, band), cols])
             + _dot(p_buf[slot, :, band:], vc_ref[0, :, cols]))
        return o / denom

    scores(0)
    outs = []
    for head in range(N_ATT_HEADS):
        if head + 1 < N_ATT_HEADS:
            scores(head + 1)
        outs.append(values(head, softmax(head)))
        if head % HEAD_PAIR == HEAD_PAIR - 1:
            o_ref[0, :, head_cols(head)] = jnp.where(lane < HEAD_DIM, outs[-2], outs[-1]).astype(o_ref.dtype)


def _attention(q, k, v, k_ctx, v_ctx, tbl):
    b, t, _ = q.shape
    n_ctx = k_ctx.shape[1]
    rows = t // GRID_W
    tq = ATT_Q_ROWS * GRID_W
    n_keys = ATT_BAND_ROWS * GRID_W + n_ctx
    full = lambda n: pl.BlockSpec((1, n, D_ATT), lambda bi, i: (bi, 0, 0))
    blk = pl.BlockSpec((1, tq, D_ATT), lambda bi, i: (bi, i, 0))
    return pl.pallas_call(
        functools.partial(_attn_kernel, rows=rows),
        grid=(b, rows // ATT_Q_ROWS),
        in_specs=[blk, full(t), full(t), full(n_ctx), full(n_ctx), _resident(tbl.shape)],
        out_specs=blk,
        out_shape=jax.ShapeDtypeStruct((b, t, D_ATT), BF16),
        scratch_shapes=[pltpu.VMEM((2, tq, n_keys), F32), pltpu.VMEM((2, tq, n_keys), BF16)],
        compiler_params=pltpu.CompilerParams(
            dimension_semantics=("parallel", "arbitrary"), vmem_limit_bytes=VMEM_LIMIT_BYTES),
        name="attn",
    )(q, k, v, k_ctx, v_ctx, tbl)


def _bias_tables(rpb, rows):
    n_blk = rows // ATT_Q_ROWS
    n_off = 2 * WIN_ROWS - 1
    pad = ATT_BAND_ROWS
    n_pairs = n_off + 2 * pad
    col = np.arange(GRID_W)
    col_off = np.clip(col[None, :] - col[:, None] + WIN_COLS - 1, 0, 2 * WIN_COLS - 2)
    n_col_off = 2 * WIN_COLS - 1
    rp = jnp.pad(rpb, ((0, 0), (pad, pad + 1), (0, 0)))
    z = jnp.concatenate([rp[:, :n_pairs], rp[:, 1:n_pairs + 1]], axis=-1)
    m = np.arange(2 * GRID_W)
    f = np.arange(2 * n_col_off)
    select = ((f[:, None, None] // n_col_off == m[None, None, :] // GRID_W)
              & (f[:, None, None] % n_col_off == col_off[:, m % GRID_W][None])).astype(np.float32)
    pairs = jnp.einsum("hRf,fqm->hRqm", z, select, precision=lax.Precision.HIGHEST)

    plan = []
    for i in (0, 1, n_blk - 1):
        band_start = int(np.clip(ATT_Q_ROWS * i - WIN_ROWS // 2, 0, rows - ATT_BAND_ROWS))
        per_row = []
        for rq in range(ATT_Q_ROWS):
            r = ATT_Q_ROWS * i + rq
            kr = band_start + np.arange(ATT_BAND_ROWS)
            win_start = int(np.clip(r - WIN_ROWS // 2, 0, rows - WIN_ROWS))
            row_valid = tuple(bool(ok) for ok in (kr >= win_start) & (kr < win_start + WIN_ROWS))
            first_off = band_start - r + WIN_ROWS - 1
            assert -pad <= first_off <= n_off
            per_row.append((first_off + pad, row_valid))
        plan.append(tuple(per_row))

    tq, band = ATT_Q_ROWS * GRID_W, ATT_BAND_ROWS * GRID_W
    return pl.pallas_call(
        functools.partial(_table_kernel, plan=tuple(plan)),
        grid=(N_ATT_HEADS,),
        in_specs=[pl.BlockSpec((1, n_pairs, GRID_W, 2 * GRID_W), lambda h: (h, 0, 0, 0))],
        out_specs=pl.BlockSpec((3, 1, tq, band), lambda h: (0, h, 0, 0)),
        out_shape=jax.ShapeDtypeStruct((3, N_ATT_HEADS, tq, band), F32),
        name="bias_table",
    )(pairs)


def _table_kernel(pairs_ref, o_ref, *, plan):
    qc = lax.broadcasted_iota(jnp.int32, (GRID_W, 2 * GRID_W), 0)
    m = lax.broadcasted_iota(jnp.int32, (GRID_W, 2 * GRID_W), 1)
    kc = jnp.where(m < GRID_W, m, m - GRID_W)
    col_start = jnp.clip(qc - WIN_COLS // 2, 0, GRID_W - WIN_COLS)
    col_ok = (kc >= col_start) & (kc < col_start + WIN_COLS)
    first_half = m < GRID_W
    masked = jnp.full((GRID_W, 2 * GRID_W), NEG_INF, F32)
    for kind, per_row in enumerate(plan):
        for rq, (lo, row_valid) in enumerate(per_row):
            for jp in range(ATT_BAND_ROWS // 2):
                ok0, ok1 = row_valid[2 * jp], row_valid[2 * jp + 1]
                if ok0 and ok1:
                    ok = col_ok
                elif ok0:
                    ok = col_ok & first_half
                elif ok1:
                    ok = col_ok & jnp.logical_not(first_half)
                else:
                    ok = None
                tile = masked if ok is None else jnp.where(ok, pairs_ref[0, lo + 2 * jp], NEG_INF)
                o_ref[kind, 0, rq * GRID_W:(rq + 1) * GRID_W, jp * 2 * GRID_W:(jp + 1) * 2 * GRID_W] = tile


def kernel(x, c, ctx, c_ctx, w_mod, b_mod, norm_pre, norm_post, ffn1_w_gu, ffn1_w_down, ffn2_w_gu,
           ffn2_w_down, w_in, w_out, conv_w, conv_b, lru_wa, lru_ba, lru_wx, lru_bx, lru_lambda, na_rpb):
    assert w_mod.shape[0] == 1, "single-layer problem"
    b, t, d = x.shape
    rows = t // GRID_W
    assert d == D_MODEL and t % (ATT_Q_ROWS * GRID_W) == 0 and rows >= 3 * ATT_Q_ROWS
    assert b + 1 <= MOD_ROWS
    for n in (t, ctx.shape[1]):
        assert n % SCAN_BLOCK == 0 and n % min(GATE_CHUNK, n) == 0

    c_rows = jnp.concatenate([c, c_ctx[None], jnp.zeros((MOD_ROWS - b - 1, d), F32)], axis=0)
    mod_all = _modulation(c_rows, w_mod[0], b_mod[0])
    mod = mod_all[:b].reshape(b, N_MOD, d)
    mod_ctx = jnp.broadcast_to(mod_all[b].reshape(1, N_MOD, d), (b, N_MOD, d))

    npre, npost = norm_pre[0], norm_post[0]
    wgu1, wd1 = ffn1_w_gu[0].astype(BF16), ffn1_w_down[0].astype(BF16)
    wgu2, wd2 = ffn2_w_gu[0].astype(BF16), ffn2_w_down[0].astype(BF16)
    win, wout = w_in[0].astype(BF16), w_out[0].astype(BF16)

    q_scale = HEAD_DIM ** -0.5
    x1, xr, gr, q, k, v = _layer_in(
        x, mod, npre, npost, wgu1, wd1, win,
        proj=((0, D_REC, 1.0, F32), (C_RG, D_REC, 1.0, F32), (C_Q, D_ATT, q_scale, BF16),
              (C_K, D_ATT, 1.0, BF16), (C_V, D_ATT, 1.0, BF16)),
        write_x=True)
    xr_ctx, k_ctx, v_ctx = _layer_in(
        ctx, mod_ctx, npre, npost, wgu1, wd1, win,
        proj=((0, D_REC, 1.0, F32), (C_K, D_ATT, 1.0, BF16), (C_V, D_ATT, 1.0, BF16)),
        write_x=False)

    wg, bg, lam = _rglru_params(lru_wa[0], lru_ba[0], lru_wx[0], lru_bx[0], lru_lambda[0])
    y_rec = _rglru(xr, xr_ctx, gr, conv_w[0], conv_b[0], wg, bg, lam)
    y_att = _attention(q, k, v, k_ctx, v_ctx, _bias_tables(na_rpb[0].astype(F32), rows))
    return _layer_out(x1, y_rec, y_att, mod, npre, npost, wout, wgu2, wd2)
```

```python
import functools

import numpy as np
import jax
import jax.numpy as jnp
from jax import lax
from jax.experimental import pallas as pl
from jax.experimental.pallas import tpu as pltpu

F32 = jnp.float32
BF16 = jnp.bfloat16

D_MODEL = 1024
D_REC = 512
D_ATT = 512
HEAD_DIM = 64
N_ATT_HEADS = 8
N_REC_BLOCKS = 8
REC_BLOCK = 64
CONV_W = 4
CONV_PAD_LEFT = 2
LRU_C = 8.0
GRID_W = 64
WIN_ROWS = 8
WIN_COLS = 16
D_FF = 2816
N_MOD = 9
EPS = 1e-6
NEG_INF = -1e30
LOG2_E = 1.4426950408889634
C_RG, C_Q, C_K, C_V = 512, 1024, 1536, 2048
D_IN = 2560

LANES = 128
SUBLANES = 8
VMEM_LIMIT_BYTES = 56 * 1024 * 1024

TOKEN_TILE = 512
FF_CHUNK = 256
MOD_ROWS = 16
REC_GROUP = LANES
GATE_CHUNK = 2048
SCAN_BLOCK = 64
SCAN_UNROLL = 4
ATT_Q_ROWS = 4
ATT_BAND_ROWS = 12
ATT_STEP_BLOCKS = 4
HEAD_PAIR = LANES // HEAD_DIM
ATT_ROW_CHUNK = 32


def _sigmoid(x):
    return 0.5 * (jnp.tanh(0.5 * x) + 1.0)


def _rms(x, g):
    return x * lax.rsqrt(jnp.mean(x * x, axis=-1, keepdims=True) + EPS) * g


def _dot(a, b):
    return jnp.dot(a, b, preferred_element_type=F32)


def _resident(shape):
    return pl.BlockSpec(shape, lambda *_: (0,) * len(shape), pipeline_mode=pl.Buffered(1))


def _mod_kernel(c_ref, w_ref, b_ref, o_ref):
    c = c_ref[...]
    s = (c * _sigmoid(c)).astype(BF16)
    o_ref[...] = _dot(s, w_ref[...].astype(BF16)) + b_ref[...]


def _modulation(c_rows, w_mod, b_mod):
    n = w_mod.shape[1]
    return pl.pallas_call(
        _mod_kernel,
        grid=(n // D_MODEL,),
        in_specs=[
            pl.BlockSpec((MOD_ROWS, D_MODEL), lambda j: (0, 0)),
            pl.BlockSpec((D_MODEL, D_MODEL), lambda j: (0, j)),
            pl.BlockSpec((1, D_MODEL), lambda j: (0, j)),
        ],
        out_specs=pl.BlockSpec((MOD_ROWS, D_MODEL), lambda j: (0, j)),
        out_shape=jax.ShapeDtypeStruct((MOD_ROWS, n), F32),
        name="mod",
    )(c_rows, w_mod, b_mod.reshape(1, n))


def _swiglu(h, wgu_ref, wd_ref):
    acc = None
    for c in range(D_FF // FF_CHUNK):
        lo = c * FF_CHUNK
        g = _dot(h, wgu_ref[:, lo:lo + FF_CHUNK])
        u = _dot(h, wgu_ref[:, D_FF + lo:D_FF + lo + FF_CHUNK])
        a = (g * _sigmoid(g) * u).astype(BF16)
        d = _dot(a, wd_ref[lo:lo + FF_CHUNK, :])
        acc = d if acc is None else acc + d
    return acc


def _ffn_sublayer(x, mod_ref, npre_ref, npost_ref, idx, wgu_ref, wd_ref):
    shift = mod_ref[0, 3 * idx:3 * idx + 1, :]
    scale = mod_ref[0, 3 * idx + 1:3 * idx + 2, :]
    gate = mod_ref[0, 3 * idx + 2:3 * idx + 3, :]
    h = _rms(x, npre_ref[idx:idx + 1, :]) * (1.0 + scale) + shift
    y = _swiglu(h.astype(BF16), wgu_ref, wd_ref)
    return x + 0.5 * gate * _rms(y, npost_ref[idx:idx + 1, :])


def _layer_in_kernel(x_ref, mod_ref, npre_ref, npost_ref, wgu_ref, wd_ref, win_ref, *out_refs,
                     proj_cols, write_x):
    x1 = _ffn_sublayer(x_ref[0], mod_ref, npre_ref, npost_ref, 0, wgu_ref, wd_ref)
    outs = list(out_refs)
    if write_x:
        outs.pop(0)[0] = x1
    shift = mod_ref[0, 3:4, :]
    scale = mod_ref[0, 4:5, :]
    h = (_rms(x1, npre_ref[1:2, :]) * (1.0 + scale) + shift).astype(BF16)
    for o_ref, (lo, width, mult) in zip(outs, proj_cols):
        p = _dot(h, win_ref[:, lo:lo + width])
        if mult != 1.0:
            p = p * mult
        o_ref[0] = p.astype(o_ref.dtype)


def _layer_in(x, mod, npre, npost, wgu, wd, win, proj, write_x):
    b, t, d = x.shape
    tm = min(TOKEN_TILE, t)
    tok = lambda w: pl.BlockSpec((1, tm, w), lambda i, j: (i, j, 0))
    out_shape, out_specs = [], []
    if write_x:
        out_shape.append(jax.ShapeDtypeStruct((b, t, d), F32))
        out_specs.append(tok(d))
    for _, width, _, dtype in proj:
        out_shape.append(jax.ShapeDtypeStruct((b, t, width), dtype))
        out_specs.append(tok(width))
    return pl.pallas_call(
        functools.partial(_layer_in_kernel, proj_cols=tuple(p[:3] for p in proj), write_x=write_x),
        grid=(b, t // tm),
        in_specs=[
            tok(d),
            pl.BlockSpec((1, N_MOD, d), lambda i, j: (i, 0, 0)),
            _resident(npre.shape), _resident(npost.shape),
            _resident(wgu.shape), _resident(wd.shape), _resident(win.shape),
        ],
        out_specs=out_specs,
        out_shape=out_shape,
        compiler_params=pltpu.CompilerParams(
            dimension_semantics=("parallel", "parallel"), vmem_limit_bytes=VMEM_LIMIT_BYTES),
        name="layer_in",
    )(x, mod, npre, npost, wgu, wd, win)


def _layer_out_kernel(x_ref, yr_ref, ya_ref, mod_ref, npre_ref, npost_ref, wout_ref, wgu_ref, wd_ref,
                      o_ref):
    y = _dot(yr_ref[0], wout_ref[:D_REC, :]) + _dot(ya_ref[0], wout_ref[D_REC:, :])
    x2 = x_ref[0] + mod_ref[0, 5:6, :] * _rms(y, npost_ref[1:2, :])
    o_ref[0] = _ffn_sublayer(x2, mod_ref, npre_ref, npost_ref, 2, wgu_ref, wd_ref)


def _layer_out(x1, y_rec, y_att, mod, npre, npost, wout, wgu, wd):
    b, t, d = x1.shape
    tm = min(TOKEN_TILE, t)
    tok = lambda w: pl.BlockSpec((1, tm, w), lambda i, j: (i, j, 0))
    return pl.pallas_call(
        _layer_out_kernel,
        grid=(b, t // tm),
        in_specs=[
            tok(d), tok(D_REC), tok(D_ATT),
            pl.BlockSpec((1, N_MOD, d), lambda i, j: (i, 0, 0)),
            _resident(npre.shape), _resident(npost.shape),
            _resident(wout.shape), _resident(wgu.shape), _resident(wd.shape),
        ],
        out_specs=tok(d),
        out_shape=jax.ShapeDtypeStruct((b, t, d), F32),
        compiler_params=pltpu.CompilerParams(
            dimension_semantics=("parallel", "parallel"), vmem_limit_bytes=VMEM_LIMIT_BYTES),
        name="layer_out",
    )(x1, y_rec, y_att, mod, npre, npost, wout, wgu, wd)


def _sublane_scan(a, u, row, reverse):
    for s in (1, 2, 4):
        if reverse:
            keep = row < SUBLANES - s
            shift = SUBLANES - s
        else:
            keep = row >= s
            shift = s
        a_sh = jnp.where(keep, pltpu.roll(a, shift, 0), 1.0)
        u_sh = jnp.where(keep, pltpu.roll(u, shift, 0), 0.0)
        u = a * u_sh + u
        a = a * a_sh
    return a, u


def _scan_block(a_ref, u_ref, h_ref, t0, carry, row, reverse):
    steps = range(SUBLANES - 1, -1, -1) if reverse else range(SUBLANES)
    h = p = None
    hs, ps = [None] * SUBLANES, [None] * SUBLANES
    for r in steps:
        a = a_ref[pl.ds(t0 + r, SUBLANES, stride=SUBLANES), :]
        u = u_ref[pl.ds(t0 + r, SUBLANES, stride=SUBLANES), :]
        h, p = (u, a) if h is None else (a * h + u, a * p)
        hs[r], ps[r] = h, p
    pp, hh = _sublane_scan(p, h, row, reverse)
    after = pp * carry + hh
    if reverse:
        before = jnp.where(row < SUBLANES - 1, pltpu.roll(after, SUBLANES - 1, 0), carry)
    else:
        before = jnp.where(row >= 1, pltpu.roll(after, 1, 0), carry)
    for r in range(SUBLANES):
        h_ref[pl.ds(t0 + r, SUBLANES, stride=SUBLANES), :] = hs[r] + ps[r] * before
    last = 0 if reverse else SUBLANES - 1
    return pp[last:last + 1, :] * carry + hh[last:last + 1, :]


def _rglru_kernel(xr_ref, xc_ref, gr_ref, cw_ref, cb_ref, wg_ref, bg_ref, lam_ref, o_ref,
                  pad_s, conv_s, af_s, uf_s, ab_s, ub_s, hf_s, hb_s):
    cg = REC_GROUP
    lam = lam_ref[0]
    log_sig_lam = jnp.minimum(lam, 0.0) - jnp.log(1.0 + jnp.exp(-jnp.abs(lam)))
    neg_half_c_lsl = (-0.5 * LRU_C) * log_sig_lam
    row = lax.broadcasted_iota(jnp.int32, (SUBLANES, cg), 0)

    def conv_gates(src_ref, n):
        halo = SUBLANES
        pad_s[0:halo, :] = jnp.zeros((halo, cg), F32)
        pad_s[halo:halo + n, :] = src_ref[0]
        pad_s[halo + n:2 * halo + n, :] = jnp.zeros((halo, cg), F32)
        y = cb_ref[...]
        for k in range(CONV_W):
            off = halo + k - CONV_PAD_LEFT
            y = y + pad_s[off:off + n, :] * cw_ref[k:k + 1, :]
        conv_s[0:n, :] = y

        chunk = min(GATE_CHUNK, n)

        def gate_body(ci, _):
            r0 = pl.multiple_of(ci * chunk, chunk)
            xv = conv_s[pl.ds(r0, chunk), :]
            g = _dot(xv.astype(BF16), wg_ref[0]) + bg_ref[0]
            x_half = 0.5 * xv
            for d, (a_s, u_s) in enumerate(((af_s, uf_s), (ab_s, ub_s))):
                ta = jnp.tanh(g[:, (2 * d) * cg:(2 * d + 1) * cg])
                tx = jnp.tanh(g[:, (2 * d + 1) * cg:(2 * d + 2) * cg])
                neg_log_a = neg_half_c_lsl[:, d * cg:(d + 1) * cg] * (ta + 1.0)
                a = jnp.exp2(neg_log_a * (-LOG2_E))
                a_s[pl.ds(r0, chunk), :] = a
                one_minus_a2 = jnp.tanh(neg_log_a) * (a * a + 1.0)
                root = jnp.where(one_minus_a2 > 0.0, one_minus_a2 * lax.rsqrt(one_minus_a2), 0.0)
                u_s[pl.ds(r0, chunk), :] = root * (tx + 1.0) * x_half
            return 0

        lax.fori_loop(0, n // chunk, gate_body, 0)

    def scan(n, hf0, hb0):
        blocks = n // SCAN_BLOCK

        def body(k, carry):
            hf, hb = carry
            tf = pl.multiple_of(k * SCAN_BLOCK, SCAN_BLOCK)
            tb = pl.multiple_of((blocks - 1 - k) * SCAN_BLOCK, SCAN_BLOCK)
            hf = _scan_block(af_s, uf_s, hf_s, tf, hf, row, False)
            hb = _scan_block(ab_s, ub_s, hb_s, tb, hb, row, True)
            return hf, hb

        return lax.fori_loop(0, blocks, body, (hf0, hb0), unroll=max(1, min(SCAN_UNROLL, blocks // 2)))

    n_ctx = xc_ref.shape[1]
    n_lat = xr_ref.shape[1]
    zero = jnp.zeros((1, cg), F32)
    conv_gates(xc_ref, n_ctx)
    hf, hb = scan(n_ctx, zero, zero)
    conv_gates(xr_ref, n_lat)
    scan(n_lat, hf, hb)
    y = hf_s[0:n_lat, :] + hb_s[0:n_lat, :]
    o_ref[0] = (y * jax.nn.gelu(gr_ref[0])).astype(o_ref.dtype)


def _rglru(xr, xr_ctx, gr, conv_w, conv_b, wg, bg, lam):
    b, t, _ = xr.shape
    n_ctx = xr_ctx.shape[1]
    cg = REC_GROUP
    seq = lambda n: pl.BlockSpec((1, n, cg), lambda i, j: (i, 0, j))
    grp = lambda shape: pl.BlockSpec((1,) + shape, lambda i, j: (j, 0, 0))
    return pl.pallas_call(
        _rglru_kernel,
        grid=(b, D_REC // cg),
        in_specs=[
            seq(t), seq(n_ctx), seq(t),
            pl.BlockSpec((CONV_W, cg), lambda i, j: (0, j)),
            pl.BlockSpec((1, cg), lambda i, j: (0, j)),
            grp((cg, 4 * cg)), grp((1, 4 * cg)), grp((1, 2 * cg)),
        ],
        out_specs=seq(t),
        out_shape=jax.ShapeDtypeStruct((b, t, D_REC), BF16),
        scratch_shapes=[pltpu.VMEM((t + 2 * SUBLANES, cg), F32)] + [pltpu.VMEM((t, cg), F32)] * 7,
        compiler_params=pltpu.CompilerParams(
            dimension_semantics=("parallel", "parallel"), vmem_limit_bytes=VMEM_LIMIT_BYTES),
        name="rglru",
    )(xr, xr_ctx, gr, conv_w, conv_b.reshape(1, D_REC), wg, bg, lam)


def _rglru_params(lru_wa, lru_ba, lru_wx, lru_bx, lru_lambda):
    cg = REC_GROUP
    n_grp = D_REC // cg
    per = cg // REC_BLOCK

    def block_diag(w):
        w = w.reshape(n_grp, per, REC_BLOCK, REC_BLOCK)
        eye = jnp.eye(per, dtype=w.dtype)
        return jnp.einsum("gpcd,pq->gpcqd", w, eye).reshape(n_grp, cg, cg)

    wg = (0.5 * jnp.concatenate([block_diag(lru_wa[0]), block_diag(lru_wx[0]),
                                 block_diag(lru_wa[1]), block_diag(lru_wx[1])], axis=-1)).astype(BF16)
    vec = lambda v: v.reshape(n_grp, 1, cg)
    bg = 0.5 * jnp.concatenate([vec(lru_ba[0]), vec(lru_bx[0]), vec(lru_ba[1]), vec(lru_bx[1])], axis=-1)
    lam = jnp.concatenate([vec(lru_lambda[0]), vec(lru_lambda[1])], axis=-1)
    return wg, bg, lam


def _attn_kernel(q_ref, k_ref, v_ref, kc_ref, vc_ref, tbl_ref, o_ref, s_buf, p_buf, *, rows):
    def block(j, _):
        _attn_block(pl.program_id(1) * ATT_STEP_BLOCKS + j, pl.multiple_of(j * ATT_Q_ROWS * GRID_W, GRID_W),
                    q_ref, k_ref, v_ref, kc_ref, vc_ref, tbl_ref, o_ref, s_buf, p_buf, rows)
        return 0

    lax.fori_loop(0, ATT_STEP_BLOCKS, block, 0)


def _attn_block(i, q_off, q_ref, k_ref, v_ref, kc_ref, vc_ref, tbl_ref, o_ref, s_buf, p_buf, rows):
    n_blk = rows // ATT_Q_ROWS
    start = jnp.clip(ATT_Q_ROWS * i - WIN_ROWS // 2, 0, rows - ATT_BAND_ROWS)
    kind = jnp.where(i == 0, 0, jnp.where(i == n_blk - 1, 2, 1))
    off = pl.multiple_of(start * GRID_W, GRID_W)
    band = ATT_BAND_ROWS * GRID_W
    tq = ATT_Q_ROWS * GRID_W
    lane = lax.broadcasted_iota(jnp.int32, (tq, LANES), 1)
    contract_last = (((1,), (1,)), ((), ()))

    def head_cols(head):
        pair = head // HEAD_PAIR
        return slice(pair * LANES, (pair + 1) * LANES)

    def scores(head):
        cols, hh, slot = head_cols(head), head % HEAD_PAIR, head % 2
        q = q_ref[0, pl.ds(q_off, tq), cols]
        mine = (lane >= hh * HEAD_DIM) & (lane < (hh + 1) * HEAD_DIM)
        qm = jnp.where(mine, q, jnp.zeros_like(q))
        s_loc = lax.dot_general(qm, k_ref[0, pl.ds(off, band), cols], contract_last,
                                preferred_element_type=F32)
        s_buf[slot, :, :band] = s_loc
        s_buf[slot, :, band:] = lax.dot_general(qm, kc_ref[0, :, cols], contract_last,
                                                preferred_element_type=F32)

    def softmax(head):
        slot = head % 2
        sums = []
        for r0 in range(0, tq, ATT_ROW_CHUNK):
            s = jnp.concatenate(
                [s_buf[slot, r0:r0 + ATT_ROW_CHUNK, :band] + tbl_ref[kind, head, r0:r0 + ATT_ROW_CHUNK, :],
                 s_buf[slot, r0:r0 + ATT_ROW_CHUNK, band:]], axis=-1)
            e =jnp.exp(s - jnp.max(s, axis=-1, keepdims=True))
            sums.append(jnp.sum(e, axis=-1, keepdims=True))
            p_buf[slot, r0:r0 + ATT_ROW_CHUNK, :] = e.astype(BF16)
        return jnp.concatenate(sums, axis=0)

    def values(head, denom):
        cols, slot = head_cols(head), head % 2
        o = (_dot(p_buf[slot, :, :band], v_ref[0, pl.ds(off, band), cols])
             + _dot(p_buf[slot, :, band:], vc_ref[0, :, cols]))
        return o / denom

    scores(0)
    outs = []
    for head in range(N_ATT_HEADS):
        if head + 1 < N_ATT_HEADS:
            scores(head + 1)
        outs.append(values(head, softmax(head)))
        if head % HEAD_PAIR == HEAD_PAIR - 1:
            pair_out = jnp.where(lane < HEAD_DIM, outs[-2], outs[-1])
            o_ref[0, pl.ds(q_off, tq), head_cols(head)] = pair_out.astype(o_ref.dtype)


def _attention(q, k, v, k_ctx, v_ctx, tbl):
    b, t, _ = q.shape
    n_ctx = k_ctx.shape[1]
    rows = t // GRID_W
    tq = ATT_Q_ROWS * GRID_W
    n_keys = ATT_BAND_ROWS * GRID_W + n_ctx
    full = lambda n: pl.BlockSpec((1, n, D_ATT), lambda bi, i: (bi, 0, 0))
    blk = pl.BlockSpec((1, ATT_STEP_BLOCKS * tq, D_ATT), lambda bi, i: (bi, i, 0))
    return pl.pallas_call(
        functools.partial(_attn_kernel, rows=rows),
        grid=(b, rows // (ATT_Q_ROWS * ATT_STEP_BLOCKS)),
        in_specs=[blk, full(t), full(t), full(n_ctx), full(n_ctx), _resident(tbl.shape)],
        out_specs=blk,
        out_shape=jax.ShapeDtypeStruct((b, t, D_ATT), BF16),
        scratch_shapes=[pltpu.VMEM((2, tq, n_keys), F32), pltpu.VMEM((2, tq, n_keys), BF16)],
        compiler_params=pltpu.CompilerParams(
            dimension_semantics=("parallel", "arbitrary"), vmem_limit_bytes=VMEM_LIMIT_BYTES),
        name="attn",
    )(q, k, v, k_ctx, v_ctx, tbl)


def _bias_tables(rpb, rows):
    n_blk = rows // ATT_Q_ROWS
    n_off = 2 * WIN_ROWS - 1
    pad = ATT_BAND_ROWS
    n_pairs = n_off + 2 * pad
    col = np.arange(GRID_W)
    col_off = np.clip(col[None, :] - col[:, None] + WIN_COLS - 1, 0, 2 * WIN_COLS - 2)
    n_col_off = 2 * WIN_COLS - 1
    rp = jnp.pad(rpb, ((0, 0), (pad, pad + 1), (0, 0)))
    z = jnp.concatenate([rp[:, :n_pairs], rp[:, 1:n_pairs + 1]], axis=-1)
    m = np.arange(2 * GRID_W)
    f = np.arange(2 * n_col_off)
    select = ((f[:, None, None] // n_col_off == m[None, None, :] // GRID_W)
              & (f[:, None, None] % n_col_off == col_off[:, m % GRID_W][None])).astype(np.float32)
    pairs = jnp.einsum("hRf,fqm->hRqm", z, select, precision=lax.Precision.HIGHEST)

    plan = []
    for i in (0, 1, n_blk - 1):
        band_start = int(np.clip(ATT_Q_ROWS * i - WIN_ROWS // 2, 0, rows - ATT_BAND_ROWS))
        per_row = []
        for rq in range(ATT_Q_ROWS):
            r = ATT_Q_ROWS * i + rq
            kr = band_start + np.arange(ATT_BAND_ROWS)
            win_start = int(np.clip(r - WIN_ROWS // 2, 0, rows - WIN_ROWS))
            row_valid = tuple(bool(ok) for ok in (kr >= win_start) & (kr < win_start + WIN_ROWS))
            first_off = band_start - r + WIN_ROWS - 1
            assert -pad <= first_off <= n_off
            per_row.append((first_off + pad, row_valid))
        plan.append(tuple(per_row))

    tq, band = ATT_Q_ROWS * GRID_W, ATT_BAND_ROWS * GRID_W
    return pl.pallas_call(
        functools.partial(_table_kernel, plan=tuple(plan)),
        grid=(N_ATT_HEADS,),
        in_specs=[pl.BlockSpec((1, n_pairs, GRID_W, 2 * GRID_W), lambda h: (h, 0, 0, 0))],
        out_specs=pl.BlockSpec((3, 1, tq, band), lambda h: (0, h, 0, 0)),
        out_shape=jax.ShapeDtypeStruct((3, N_ATT_HEADS, tq, band), F32),
        name="bias_table",
    )(pairs)


def _table_kernel(pairs_ref, o_ref, *, plan):
    qc = lax.broadcasted_iota(jnp.int32, (GRID_W, 2 * GRID_W), 0)
    m = lax.broadcasted_iota(jnp.int32, (GRID_W, 2 * GRID_W), 1)
    kc = jnp.where(m < GRID_W, m, m - GRID_W)
    col_start = jnp.clip(qc - WIN_COLS // 2, 0, GRID_W - WIN_COLS)
    col_ok = (kc >= col_start) & (kc < col_start + WIN_COLS)
    first_half = m < GRID_W
    masked = jnp.full((GRID_W, 2 * GRID_W), NEG_INF, F32)
    for kind, per_row in enumerate(plan):
        for rq, (lo, row_valid) in enumerate(per_row):
            for jp in range(ATT_BAND_ROWS // 2):
                ok0, ok1 = row_valid[2 * jp], row_valid[2 * jp + 1]
                if ok0 and ok1:
                    ok = col_ok
                elif ok0:
                    ok = col_ok & first_half
                elif ok1:
                    ok = col_ok & jnp.logical_not(first_half)
                else:
                    ok = None
                tile = masked if ok is None else jnp.where(ok, pairs_ref[0, lo + 2 * jp], NEG_INF)
                o_ref[kind, 0, rq * GRID_W:(rq + 1) * GRID_W, jp * 2 * GRID_W:(jp + 1) * 2 * GRID_W] = tile


def kernel(x, c, ctx, c_ctx, w_mod, b_mod, norm_pre, norm_post, ffn1_w_gu, ffn1_w_down, ffn2_w_gu,
           ffn2_w_down, w_in, w_out, conv_w, conv_b, lru_wa, lru_ba, lru_wx, lru_bx, lru_lambda, na_rpb):
    assert w_mod.shape[0] == 1, "single-layer problem"
    b, t, d = x.shape
    rows = t // GRID_W
    assert d == D_MODEL and t % (ATT_STEP_BLOCKS * ATT_Q_ROWS * GRID_W) == 0 and rows >= 3 * ATT_Q_ROWS
    assert b + 1 <= MOD_ROWS
    for n in (t, ctx.shape[1]):
        assert n % SCAN_BLOCK == 0 and n % min(GATE_CHUNK, n) == 0

    c_rows = jnp.concatenate([c, c_ctx[None], jnp.zeros((MOD_ROWS - b - 1, d), F32)], axis=0)
    mod_all = _modulation(c_rows, w_mod[0], b_mod[0])
    mod = mod_all[:b].reshape(b, N_MOD, d)
    mod_ctx = jnp.broadcast_to(mod_all[b].reshape(1, N_MOD, d), (b, N_MOD, d))

    npre, npost = norm_pre[0], norm_post[0]
    wgu1, wd1 = ffn1_w_gu[0].astype(BF16), ffn1_w_down[0].astype(BF16)
    wgu2, wd2 = ffn2_w_gu[0].astype(BF16), ffn2_w_down[0].astype(BF16)
    win, wout = w_in[0].astype(BF16), w_out[0].astype(BF16)

    q_scale = HEAD_DIM ** -0.5
    x1, xr, gr, q, k, v = _layer_in(
        x, mod, npre, npost, wgu1, wd1, win,
        proj=((0, D_REC, 1.0, F32), (C_RG, D_REC, 1.0, F32), (C_Q, D_ATT, q_scale, BF16),
              (C_K, D_ATT, 1.0, BF16), (C_V, D_ATT, 1.0, BF16)),
        write_x=True)
    xr_ctx, k_ctx, v_ctx = _layer_in(
        ctx, mod_ctx, npre, npost, wgu1, wd1, win,
        proj=((0, D_REC, 1.0, F32), (C_K, D_ATT, 1.0, BF16), (C_V, D_ATT, 1.0, BF16)),
        write_x=False)

    wg, bg, lam = _rglru_params(lru_wa[0], lru_ba[0], lru_wx[0], lru_bx[0], lru_lambda[0])
    y_rec = _rglru(xr, xr_ctx, gr, conv_w[0], conv_b[0], wg, bg, lam)
    y_att = _attention(q, k, v, k_ctx, v_ctx, _bias_tables(na_rpb[0].astype(F32), rows))
    return _layer_out(x1, y_rec, y_att, mod, npre, npost, wout, wgu2, wd2)
```

```python
import functools

import numpy as np
import jax
import jax.numpy as jnp
from jax import lax
from jax.experimental import pallas as pl
from jax.experimental.pallas import tpu as pltpu

F32 = jnp.float32
BF16 = jnp.bfloat16

D_MODEL = 1024
D_REC = 512
D_ATT = 512
HEAD_DIM = 64
N_ATT_HEADS = 8
N_REC_BLOCKS = 8
REC_BLOCK = 64
CONV_W = 4
CONV_PAD_LEFT = 2
LRU_C = 8.0
GRID_W = 64
WIN_ROWS = 8
WIN_COLS = 16
D_FF = 2816
N_MOD = 9
EPS = 1e-6
NEG_INF = -1e30
LOG2_E = 1.4426950408889634
C_RG, C_Q, C_K, C_V = 512, 1024, 1536, 2048
D_IN = 2560

LANES = 128
SUBLANES = 8
VMEM_LIMIT_BYTES = 56 * 1024 * 1024

TOKEN_TILE = 512
OUT_TOKEN_TILE = 1024
FF_CHUNK = 256
MOD_ROWS = 16
REC_GROUP = LANES
GATE_CHUNK = 2048
SCAN_BLOCK = 64
SCAN_UNROLL = 4
ATT_Q_ROWS = 4
ATT_BAND_ROWS = 12
ATT_STEP_BLOCKS = 4
HEAD_PAIR = LANES // HEAD_DIM
ATT_ROW_CHUNK = 32


def _sigmoid(x):
    return 0.5 * (jnp.tanh(0.5 * x) + 1.0)


def _rms(x, g):
    return x * lax.rsqrt(jnp.mean(x * x, axis=-1, keepdims=True) + EPS) * g


def _dot(a, b):
    return jnp.dot(a, b, preferred_element_type=F32)


def _resident(shape):
    return pl.BlockSpec(shape, lambda *_: (0,) * len(shape), pipeline_mode=pl.Buffered(1))


def _mod_kernel(c_ref, w_ref, b_ref, o_ref):
    c = c_ref[...]
    s = (c * _sigmoid(c)).astype(BF16)
    o_ref[...] = _dot(s, w_ref[...].astype(BF16)) + b_ref[...]


def _modulation(c_rows, w_mod, b_mod):
    n = w_mod.shape[1]
    return pl.pallas_call(
        _mod_kernel,
        grid=(n // D_MODEL,),
        in_specs=[
            pl.BlockSpec((MOD_ROWS, D_MODEL), lambda j: (0, 0)),
            pl.BlockSpec((D_MODEL, D_MODEL), lambda j: (0, j)),
            pl.BlockSpec((1, D_MODEL), lambda j: (0, j)),
        ],
        out_specs=pl.BlockSpec((MOD_ROWS, D_MODEL), lambda j: (0, j)),
        out_shape=jax.ShapeDtypeStruct((MOD_ROWS, n), F32),
        name="mod",
    )(c_rows, w_mod, b_mod.reshape(1, n))


def _swiglu(h, wgu_ref, wd_ref):
    acc = None
    for c in range(D_FF // FF_CHUNK):
        lo = c * FF_CHUNK
        g = _dot(h, wgu_ref[:, lo:lo + FF_CHUNK])
        u = _dot(h, wgu_ref[:, D_FF + lo:D_FF + lo + FF_CHUNK])
        a = (g * _sigmoid(g) * u).astype(BF16)
        d = _dot(a, wd_ref[lo:lo + FF_CHUNK, :])
        acc = d if acc is None else acc + d
    return acc


def _ffn_sublayer(x, mod_ref, npre_ref, npost_ref, idx, wgu_ref, wd_ref):
    shift = mod_ref[0, 3 * idx:3 * idx + 1, :]
    scale = mod_ref[0, 3 * idx + 1:3 * idx + 2, :]
    gate = mod_ref[0, 3 * idx + 2:3 * idx + 3, :]
    h = _rms(x, npre_ref[idx:idx + 1, :]) * (1.0 + scale) + shift
    y = _swiglu(h.astype(BF16), wgu_ref, wd_ref)
    return x + 0.5 * gate * _rms(y, npost_ref[idx:idx + 1, :])


def _layer_in_kernel(x_ref, mod_ref, npre_ref, npost_ref, wgu_ref, wd_ref, win_ref, *out_refs,
                     proj_cols, write_x):
    x1 = _ffn_sublayer(x_ref[0], mod_ref, npre_ref, npost_ref, 0, wgu_ref, wd_ref)
    outs = list(out_refs)
    if write_x:
        outs.pop(0)[0] = x1
    shift = mod_ref[0, 3:4, :]
    scale = mod_ref[0, 4:5, :]
    h = (_rms(x1, npre_ref[1:2, :]) * (1.0 + scale) + shift).astype(BF16)
    for o_ref, (lo, width, mult) in zip(outs, proj_cols):
        p = _dot(h, win_ref[:, lo:lo + width])
        if mult != 1.0:
            p = p * mult
        o_ref[0] = p.astype(o_ref.dtype)


def _layer_in(x, mod, npre, npost, wgu, wd, win, proj, write_x):
    b, t, d = x.shape
    tm = min(TOKEN_TILE, t)
    assert t % tm == 0
    tok = lambda w: pl.BlockSpec((1, tm, w), lambda i, j: (i, j, 0))
    out_shape, out_specs = [], []
    if write_x:
        out_shape.append(jax.ShapeDtypeStruct((b, t, d), F32))
        out_specs.append(tok(d))
    for _, width, _, dtype in proj:
        out_shape.append(jax.ShapeDtypeStruct((b, t, width), dtype))
        out_specs.append(tok(width))
    return pl.pallas_call(
        functools.partial(_layer_in_kernel, proj_cols=tuple(p[:3] for p in proj), write_x=write_x),
        grid=(b, t // tm),
        in_specs=[
            tok(d),
            pl.BlockSpec((1, N_MOD, d), lambda i, j: (i, 0, 0)),
            _resident(npre.shape), _resident(npost.shape),
            _resident(wgu.shape), _resident(wd.shape), _resident(win.shape),
        ],
        out_specs=out_specs,
        out_shape=out_shape,
        compiler_params=pltpu.CompilerParams(
            dimension_semantics=("parallel", "parallel"), vmem_limit_bytes=VMEM_LIMIT_BYTES),
        name="layer_in",
    )(x, mod, npre, npost, wgu, wd, win)


def _layer_out_kernel(x_ref, yr_ref, ya_ref, mod_ref, npre_ref, npost_ref, wout_ref, wgu_ref, wd_ref,
                      o_ref):
    y = _dot(yr_ref[0], wout_ref[:D_REC, :]) + _dot(ya_ref[0], wout_ref[D_REC:, :])
    x2 = x_ref[0] + mod_ref[0, 5:6, :] * _rms(y, npost_ref[1:2, :])
    o_ref[0] = _ffn_sublayer(x2, mod_ref, npre_ref, npost_ref, 2, wgu_ref, wd_ref)


def _layer_out(x1, y_rec, y_att, mod, npre, npost, wout, wgu, wd):
    b, t, d = x1.shape
    tm = min(OUT_TOKEN_TILE, t)
    assert t % tm == 0
    tok = lambda w: pl.BlockSpec((1, tm, w), lambda i, j: (i, j, 0))
    return pl.pallas_call(
        _layer_out_kernel,
        grid=(b, t // tm),
        in_specs=[
            tok(d), tok(D_REC), tok(D_ATT),
            pl.BlockSpec((1, N_MOD, d), lambda i, j: (i, 0, 0)),
            _resident(npre.shape), _resident(npost.shape),
            _resident(wout.shape), _resident(wgu.shape), _resident(wd.shape),
        ],
        out_specs=tok(d),
        out_shape=jax.ShapeDtypeStruct((b, t, d), F32),
        compiler_params=pltpu.CompilerParams(
            dimension_semantics=("parallel", "parallel"), vmem_limit_bytes=VMEM_LIMIT_BYTES),
        name="layer_out",
    )(x1, y_rec, y_att, mod, npre, npost, wout, wgu, wd)


def _sublane_scan(a, u, row, reverse):
    for s in (1, 2, 4):
        if reverse:
            keep = row < SUBLANES - s
            shift = SUBLANES - s
        else:
            keep = row >= s
            shift = s
        a_sh = jnp.where(keep, pltpu.roll(a, shift, 0), 1.0)
        u_sh = jnp.where(keep, pltpu.roll(u, shift, 0), 0.0)
        u = a * u_sh + u
        a = a * a_sh
    return a, u


def _scan_block(a_ref, u_ref, h_ref, t0, carry, row, reverse):
    steps = range(SUBLANES - 1, -1, -1) if reverse else range(SUBLANES)
    h = p = None
    hs, ps = [None] * SUBLANES, [None] * SUBLANES
    for r in steps:
        a = a_ref[pl.ds(t0 + r, SUBLANES, stride=SUBLANES), :]
        u = u_ref[pl.ds(t0 + r, SUBLANES, stride=SUBLANES), :]
        h, p = (u, a) if h is None else (a * h + u, a * p)
        hs[r], ps[r] = h, p
    pp, hh = _sublane_scan(p, h, row, reverse)
    after = pp * carry + hh
    if reverse:
        before = jnp.where(row < SUBLANES - 1, pltpu.roll(after, SUBLANES - 1, 0), carry)
    else:
        before = jnp.where(row >= 1, pltpu.roll(after, 1, 0), carry)
    for r in range(SUBLANES):
        h_ref[pl.ds(t0 + r, SUBLANES, stride=SUBLANES), :] = hs[r] + ps[r] * before
    last = 0 if reverse else SUBLANES - 1
    return pp[last:last + 1, :] * carry + hh[last:last + 1, :]


def _rglru_kernel(xr_ref, xc_ref, gr_ref, cw_ref, cb_ref, wg_ref, bg_ref, lam_ref, o_ref,
                  pad_s, conv_s, af_s, uf_s, ab_s, ub_s, hf_s, hb_s):
    cg = REC_GROUP
    lam = lam_ref[0]
    log_sig_lam = jnp.minimum(lam, 0.0) - jnp.log(1.0 + jnp.exp(-jnp.abs(lam)))
    neg_half_c_lsl = (-0.5 * LRU_C) * log_sig_lam
    row = lax.broadcasted_iota(jnp.int32, (SUBLANES, cg), 0)

    def conv_gates(src_ref, n):
        halo = SUBLANES
        pad_s[0:halo, :] = jnp.zeros((halo, cg), F32)
        pad_s[halo:halo + n, :] = src_ref[0]
        pad_s[halo + n:2 * halo + n, :] = jnp.zeros((halo, cg), F32)
        y = cb_ref[...]
        for k in range(CONV_W):
            off = halo + k - CONV_PAD_LEFT
            y = y + pad_s[off:off + n, :] * cw_ref[k:k + 1, :]
        conv_s[0:n, :] = y

        chunk = min(GATE_CHUNK, n)

        def gate_body(ci, _):
            r0 = pl.multiple_of(ci * chunk, chunk)
            xv = conv_s[pl.ds(r0, chunk), :]
            g = _dot(xv.astype(BF16), wg_ref[0]) + bg_ref[0]
            x_half = 0.5 * xv
            for d, (a_s, u_s) in enumerate(((af_s, uf_s), (ab_s, ub_s))):
                ta = jnp.tanh(g[:, (2 * d) * cg:(2 * d + 1) * cg])
                tx = jnp.tanh(g[:, (2 * d + 1) * cg:(2 * d + 2) * cg])
                neg_log_a = neg_half_c_lsl[:, d * cg:(d + 1) * cg] * (ta + 1.0)
                a = jnp.exp2(neg_log_a * (-LOG2_E))
                a_s[pl.ds(r0, chunk), :] = a
                one_minus_a2 = jnp.tanh(neg_log_a) * (a * a + 1.0)
                root = jnp.where(one_minus_a2 > 0.0, one_minus_a2 * lax.rsqrt(one_minus_a2), 0.0)
                u_s[pl.ds(r0, chunk), :] = root * (tx + 1.0) * x_half
            return 0

        lax.fori_loop(0, n // chunk, gate_body, 0)

    def scan(n, hf0, hb0):
        blocks = n // SCAN_BLOCK

        def body(k, carry):
            hf, hb = carry
            tf = pl.multiple_of(k * SCAN_BLOCK, SCAN_BLOCK)
            tb = pl.multiple_of((blocks - 1 - k) * SCAN_BLOCK, SCAN_BLOCK)
            hf = _scan_block(af_s, uf_s, hf_s, tf, hf, row, False)
            hb = _scan_block(ab_s, ub_s, hb_s, tb, hb, row, True)
            return hf, hb

        return lax.fori_loop(0, blocks, body, (hf0, hb0), unroll=max(1, min(SCAN_UNROLL, blocks // 2)))

    n_ctx = xc_ref.shape[1]
    n_lat = xr_ref.shape[1]
    zero = jnp.zeros((1, cg), F32)
    conv_gates(xc_ref, n_ctx)
    hf, hb = scan(n_ctx, zero, zero)
    conv_gates(xr_ref, n_lat)
    scan(n_lat, hf, hb)
    y = hf_s[0:n_lat, :] + hb_s[0:n_lat, :]
    o_ref[0] = (y * jax.nn.gelu(gr_ref[0])).astype(o_ref.dtype)


def _rglru(xr, xr_ctx, gr, conv_w, conv_b, wg, bg, lam):
    b, t, _ = xr.shape
    n_ctx = xr_ctx.shape[1]
    cg = REC_GROUP
    seq = lambda n: pl.BlockSpec((1, n, cg), lambda i, j: (i, 0, j))
    grp = lambda shape: pl.BlockSpec((1,) + shape, lambda i, j: (j, 0, 0))
    return pl.pallas_call(
        _rglru_kernel,
        grid=(b, D_REC // cg),
        in_specs=[
            seq(t), seq(n_ctx), seq(t),
            pl.BlockSpec((CONV_W, cg), lambda i, j: (0, j)),
            pl.BlockSpec((1, cg), lambda i, j: (0, j)),
            grp((cg, 4 * cg)), grp((1, 4 * cg)), grp((1, 2 * cg)),
        ],
        out_specs=seq(t),
        out_shape=jax.ShapeDtypeStruct((b, t, D_REC), BF16),
        scratch_shapes=[pltpu.VMEM((t + 2 * SUBLANES, cg), F32)] + [pltpu.VMEM((t, cg), F32)] * 7,
        compiler_params=pltpu.CompilerParams(
            dimension_semantics=("parallel", "parallel"), vmem_limit_bytes=VMEM_LIMIT_BYTES),
        name="rglru",
    )(xr, xr_ctx, gr, conv_w, conv_b.reshape(1, D_REC), wg, bg, lam)


def _rglru_params(lru_wa, lru_ba, lru_wx, lru_bx, lru_lambda):
    cg = REC_GROUP
    n_grp = D_REC // cg
    per = cg // REC_BLOCK

    def block_diag(w):
        w = w.reshape(n_grp, per, REC_BLOCK, REC_BLOCK)
        eye = jnp.eye(per, dtype=w.dtype)
        return jnp.einsum("gpcd,pq->gpcqd", w, eye).reshape(n_grp, cg, cg)

    wg = (0.5 * jnp.concatenate([block_diag(lru_wa[0]), block_diag(lru_wx[0]),
                                 block_diag(lru_wa[1]), block_diag(lru_wx[1])], axis=-1)).astype(BF16)
    vec = lambda v: v.reshape(n_grp, 1, cg)
    bg = 0.5 * jnp.concatenate([vec(lru_ba[0]), vec(lru_bx[0]), vec(lru_ba[1]), vec(lru_bx[1])], axis=-1)
    lam = jnp.concatenate([vec(lru_lambda[0]), vec(lru_lambda[1])], axis=-1)
    return wg, bg, lam


def _attn_kernel(q_ref, k_ref, v_ref, kc_ref, vc_ref, tbl_ref, o_ref, s_buf, p_buf, *, rows):
    def block(j, _):
        _attn_block(pl.program_id(1) * ATT_STEP_BLOCKS + j, pl.multiple_of(j * ATT_Q_ROWS * GRID_W, GRID_W),
                    q_ref, k_ref, v_ref, kc_ref, vc_ref, tbl_ref, o_ref, s_buf, p_buf, rows)
        return 0

    lax.fori_loop(0, ATT_STEP_BLOCKS, block, 0)


def _attn_block(i, q_off, q_ref, k_ref, v_ref, kc_ref, vc_ref, tbl_ref, o_ref, s_buf, p_buf, rows):
    n_blk = rows // ATT_Q_ROWS
    start = jnp.clip(ATT_Q_ROWS * i - WIN_ROWS // 2, 0, rows - ATT_BAND_ROWS)
    kind = jnp.where(i == 0, 0, jnp.where(i == n_blk - 1, 2, 1))
    off = pl.multiple_of(start * GRID_W, GRID_W)
    band = ATT_BAND_ROWS * GRID_W
    tq = ATT_Q_ROWS * GRID_W
    lane = lax.broadcasted_iota(jnp.int32, (tq, LANES), 1)
    contract_last = (((1,), (1,)), ((), ()))

    def head_cols(head):
        pair = head // HEAD_PAIR
        return slice(pair * LANES, (pair + 1) * LANES)

    def scores(head):
        cols, hh, slot = head_cols(head), head % HEAD_PAIR, head % 2
        q = q_ref[0, pl.ds(q_off, tq), cols]
        mine = (lane >= hh * HEAD_DIM) & (lane < (hh + 1) * HEAD_DIM)
        qm = jnp.where(mine, q, jnp.zeros_like(q))
        s_loc = lax.dot_general(qm, k_ref[0, pl.ds(off, band), cols], contract_last,
                                preferred_element_type=F32)
        s_buf[slot, :, :band] = s_loc
        s_buf[slot, :, band:] = lax.dot_general(qm, kc_ref[0, :, cols], contract_last,
                                                preferred_element_type=F32)

    def softmax(head):
        slot = head % 2
        sums = []
        for r0 in range(0, tq, ATT_ROW_CHUNK):
            s = jnp.concatenate(
                [s_buf[slot, r0:r0 + ATT_ROW_CHUNK, :band] + tbl_ref[kind, head, r0:r0 + ATT_ROW_CHUNK, :],
                 s_buf[slot, r0:r0 + ATT_ROW_CHUNK, band:]], axis=-1)
            e =jnp.exp(s - jnp.max(s, axis=-1, keepdims=True))
            sums.append(jnp.sum(e, axis=-1, keepdims=True))
            p_buf[slot, r0:r0 + ATT_ROW_CHUNK, :] = e.astype(BF16)
        return jnp.concatenate(sums, axis=0)

    def values(head, denom):
        cols, slot = head_cols(head), head % 2
        o = (_dot(p_buf[slot, :, :band], v_ref[0, pl.ds(off, band), cols])
             + _dot(p_buf[slot, :, band:], vc_ref[0, :, cols]))
        return o / denom

    scores(0)
    outs = []
    for head in range(N_ATT_HEADS):
        if head + 1 < N_ATT_HEADS:
            scores(head + 1)
        outs.append(values(head, softmax(head)))
        if head % HEAD_PAIR == HEAD_PAIR - 1:
            pair_out = jnp.where(lane < HEAD_DIM, outs[-2], outs[-1])
            o_ref[0, pl.ds(q_off, tq), head_cols(head)] = pair_out.astype(o_ref.dtype)


def _attention(q, k, v, k_ctx, v_ctx, tbl):
    b, t, _ = q.shape
    n_ctx = k_ctx.shape[1]
    rows = t // GRID_W
    tq = ATT_Q_ROWS * GRID_W
    n_keys = ATT_BAND_ROWS * GRID_W + n_ctx
    full = lambda n: pl.BlockSpec((1, n, D_ATT), lambda bi, i: (bi, 0, 0))
    blk = pl.BlockSpec((1, ATT_STEP_BLOCKS * tq, D_ATT), lambda bi, i: (bi, i, 0))
    return pl.pallas_call(
        functools.partial(_attn_kernel, rows=rows),
        grid=(b, rows // (ATT_Q_ROWS * ATT_STEP_BLOCKS)),
        in_specs=[blk, full(t), full(t), full(n_ctx), full(n_ctx), _resident(tbl.shape)],
        out_specs=blk,
        out_shape=jax.ShapeDtypeStruct((b, t, D_ATT), BF16),
        scratch_shapes=[pltpu.VMEM((2, tq, n_keys), F32), pltpu.VMEM((2, tq, n_keys), BF16)],
        compiler_params=pltpu.CompilerParams(
            dimension_semantics=("parallel", "arbitrary"), vmem_limit_bytes=VMEM_LIMIT_BYTES),
        name="attn",
    )(q, k, v, k_ctx, v_ctx, tbl)


def _bias_tables(rpb, rows):
    n_blk = rows // ATT_Q_ROWS
    n_off = 2 * WIN_ROWS - 1
    pad = ATT_BAND_ROWS
    n_pairs = n_off + 2 * pad
    col = np.arange(GRID_W)
    col_off = np.clip(col[None, :] - col[:, None] + WIN_COLS - 1, 0, 2 * WIN_COLS - 2)
    n_col_off = 2 * WIN_COLS - 1
    rp = jnp.pad(rpb, ((0, 0), (pad, pad + 1), (0, 0)))
    z = jnp.concatenate([rp[:, :n_pairs], rp[:, 1:n_pairs + 1]], axis=-1)
    m = np.arange(2 * GRID_W)
    f = np.arange(2 * n_col_off)
    select = ((f[:, None, None] // n_col_off == m[None, None, :] // GRID_W)
              & (f[:, None, None] % n_col_off == col_off[:, m % GRID_W][None])).astype(np.float32)
    pairs = jnp.einsum("hRf,fqm->hRqm", z, select, precision=lax.Precision.HIGHEST)

    plan = []
    for i in (0, 1, n_blk - 1):
        band_start = int(np.clip(ATT_Q_ROWS * i - WIN_ROWS // 2, 0, rows - ATT_BAND_ROWS))
        per_row = []
        for rq in range(ATT_Q_ROWS):
            r = ATT_Q_ROWS * i + rq
            kr = band_start + np.arange(ATT_BAND_ROWS)
            win_start = int(np.clip(r - WIN_ROWS // 2, 0, rows - WIN_ROWS))
            row_valid = tuple(bool(ok) for ok in (kr >= win_start) & (kr < win_start + WIN_ROWS))
            first_off = band_start - r + WIN_ROWS - 1
            assert -pad <= first_off <= n_off
            per_row.append((first_off + pad, row_valid))
        plan.append(tuple(per_row))

    tq, band = ATT_Q_ROWS * GRID_W, ATT_BAND_ROWS * GRID_W
    return pl.pallas_call(
        functools.partial(_table_kernel, plan=tuple(plan)),
        grid=(N_ATT_HEADS,),
        in_specs=[pl.BlockSpec((1, n_pairs, GRID_W, 2 * GRID_W), lambda h: (h, 0, 0, 0))],
        out_specs=pl.BlockSpec((3, 1, tq, band), lambda h: (0, h, 0, 0)),
        out_shape=jax.ShapeDtypeStruct((3, N_ATT_HEADS, tq, band), F32),
        name="bias_table",
    )(pairs)


def _table_kernel(pairs_ref, o_ref, *, plan):
    qc = lax.broadcasted_iota(jnp.int32, (GRID_W, 2 * GRID_W), 0)
    m = lax.broadcasted_iota(jnp.int32, (GRID_W, 2 * GRID_W), 1)
    kc = jnp.where(m < GRID_W, m, m - GRID_W)
    col_start = jnp.clip(qc - WIN_COLS // 2, 0, GRID_W - WIN_COLS)
    col_ok = (kc >= col_start) & (kc < col_start + WIN_COLS)
    first_half = m < GRID_W
    masked = jnp.full((GRID_W, 2 * GRID_W), NEG_INF, F32)
    for kind, per_row in enumerate(plan):
        for rq, (lo, row_valid) in enumerate(per_row):
            for jp in range(ATT_BAND_ROWS // 2):
                ok0, ok1 = row_valid[2 * jp], row_valid[2 * jp + 1]
                if ok0 and ok1:
                    ok = col_ok
                elif ok0:
                    ok = col_ok & first_half
                elif ok1:
                    ok = col_ok & jnp.logical_not(first_half)
                else:
                    ok = None
                tile = masked if ok is None else jnp.where(ok, pairs_ref[0, lo + 2 * jp], NEG_INF)
                o_ref[kind, 0, rq * GRID_W:(rq + 1) * GRID_W, jp * 2 * GRID_W:(jp + 1) * 2 * GRID_W] = tile


def kernel(x, c, ctx, c_ctx, w_mod, b_mod, norm_pre, norm_post, ffn1_w_gu, ffn1_w_down, ffn2_w_gu,
           ffn2_w_down, w_in, w_out, conv_w, conv_b, lru_wa, lru_ba, lru_wx, lru_bx, lru_lambda, na_rpb):
    assert w_mod.shape[0] == 1, "single-layer problem"
    b, t, d = x.shape
    rows = t // GRID_W
    assert d == D_MODEL and t % (ATT_STEP_BLOCKS * ATT_Q_ROWS * GRID_W) == 0 and rows >= 3 * ATT_Q_ROWS
    assert b + 1 <= MOD_ROWS
    for n in (t, ctx.shape[1]):
        assert n % SCAN_BLOCK == 0 and n % min(GATE_CHUNK, n) == 0

    c_rows = jnp.concatenate([c, c_ctx[None], jnp.zeros((MOD_ROWS - b - 1, d), F32)], axis=0)
    mod_all = _modulation(c_rows, w_mod[0], b_mod[0])
    mod = mod_all[:b].reshape(b, N_MOD, d)
    mod_ctx = mod_all[b].reshape(1, N_MOD, d)

    npre, npost = norm_pre[0], norm_post[0]
    wgu1, wd1 = ffn1_w_gu[0].astype(BF16), ffn1_w_down[0].astype(BF16)
    wgu2, wd2 = ffn2_w_gu[0].astype(BF16), ffn2_w_down[0].astype(BF16)
    win, wout = w_in[0].astype(BF16), w_out[0].astype(BF16)

    q_scale = HEAD_DIM ** -0.5
    x1, xr, gr, q, k, v = _layer_in(
        x, mod, npre, npost, wgu1, wd1, win,
        proj=((0, D_REC, 1.0, F32), (C_RG, D_REC, 1.0, F32), (C_Q, D_ATT, q_scale, BF16),
              (C_K, D_ATT, 1.0, BF16), (C_V, D_ATT, 1.0, BF16)),
        write_x=True)
    n_ctx = ctx.shape[1]
    ctx_outs = _layer_in(
        ctx.reshape(1, b * n_ctx, d), mod_ctx, npre, npost, wgu1, wd1, win,
        proj=((0, D_REC, 1.0, F32), (C_K, D_ATT, 1.0, BF16), (C_V, D_ATT, 1.0, BF16)),
        write_x=False)
    xr_ctx, k_ctx, v_ctx = (o.reshape(b, n_ctx, o.shape[-1]) for o in ctx_outs)

    wg, bg, lam = _rglru_params(lru_wa[0], lru_ba[0], lru_wx[0], lru_bx[0], lru_lambda[0])
    y_rec = _rglru(xr, xr_ctx, gr, conv_w[0], conv_b[0], wg, bg, lam)
    y_att = _attention(q, k, v, k_ctx, v_ctx, _bias_tables(na_rpb[0].astype(F32), rows))
    return _layer_out(x1, y_rec, y_att, mod, npre, npost, wout, wgu2, wd2)
```

```python
import functools

import numpy as np
import jax
import jax.numpy as jnp
from jax import lax
from jax.experimental import pallas as pl
from jax.experimental.pallas import tpu as pltpu

F32 = jnp.float32
BF16 = jnp.bfloat16

D_MODEL = 1024
D_REC = 512
D_ATT = 512
HEAD_DIM = 64
N_ATT_HEADS = 8
N_REC_BLOCKS = 8
REC_BLOCK = 64
CONV_W = 4
CONV_PAD_LEFT = 2
LRU_C = 8.0
GRID_W = 64
WIN_ROWS = 8
WIN_COLS = 16
D_FF = 2816
N_MOD = 9
EPS = 1e-6
NEG_INF = -1e30
LOG2_E = 1.4426950408889634
C_RG, C_Q, C_K, C_V = 512, 1024, 1536, 2048
D_IN = 2560

LANES = 128
SUBLANES = 8
VMEM_LIMIT_BYTES = 56 * 1024 * 1024

TOKEN_TILE = 512
OUT_TOKEN_TILE = 1024
FF_CHUNK = 256
MOD_ROWS = 16
REC_GROUP = LANES
GATE_CHUNK = 2048
SCAN_BLOCK = 64
SCAN_UNROLL = 4
ATT_Q_ROWS = 4
ATT_BAND_ROWS = 12
ATT_STEP_BLOCKS = 4
HEAD_PAIR = LANES // HEAD_DIM
ATT_ROW_CHUNK = 32


def _sigmoid(x):
    return 0.5 * (jnp.tanh(0.5 * x) + 1.0)


def _rms(x, g):
    return x * lax.rsqrt(jnp.mean(x * x, axis=-1, keepdims=True) + EPS) * g


def _dot(a, b):
    return jnp.dot(a, b, preferred_element_type=F32)


def _resident(shape):
    return pl.BlockSpec(shape, lambda *_: (0,) * len(shape), pipeline_mode=pl.Buffered(1))


def _mod_kernel(c_ref, w_ref, b_ref, o_ref):
    c = c_ref[...]
    s = (c * _sigmoid(c)).astype(BF16)
    o_ref[...] = _dot(s, w_ref[...].astype(BF16)) + b_ref[...]


def _modulation(c_rows, w_mod, b_mod):
    n = w_mod.shape[1]
    return pl.pallas_call(
        _mod_kernel,
        grid=(n // D_MODEL,),
        in_specs=[
            pl.BlockSpec((MOD_ROWS, D_MODEL), lambda j: (0, 0)),
            pl.BlockSpec((D_MODEL, D_MODEL), lambda j: (0, j)),
            pl.BlockSpec((1, D_MODEL), lambda j: (0, j)),
        ],
        out_specs=pl.BlockSpec((MOD_ROWS, D_MODEL), lambda j: (0, j)),
        out_shape=jax.ShapeDtypeStruct((MOD_ROWS, n), F32),
        name="mod",
    )(c_rows, w_mod, b_mod.reshape(1, n))


def _swiglu(h, wgu_ref, wd_ref):
    acc = None
    for c in range(D_FF // FF_CHUNK):
        lo = c * FF_CHUNK
        g = _dot(h, wgu_ref[:, lo:lo + FF_CHUNK])
        u = _dot(h, wgu_ref[:, D_FF + lo:D_FF + lo + FF_CHUNK])
        half = 0.5 * g
        a = (half * (jnp.tanh(half) + 1.0) * u).astype(BF16)
        d = _dot(a, wd_ref[lo:lo + FF_CHUNK, :])
        acc = d if acc is None else acc + d
    return acc


def _modulated_norm(x, npre_ref, mod_ref, idx):
    w = npre_ref[idx:idx + 1, :] * (1.0 + mod_ref[0, 3 * idx + 1:3 * idx + 2, :])
    return _rms(x, w) + mod_ref[0, 3 * idx:3 * idx + 1, :]


def _gated_residual(x, y, npost_ref, mod_ref, idx, res_w):
    w = (res_w * mod_ref[0, 3 * idx + 2:3 * idx + 3, :]) * npost_ref[idx:idx + 1, :]
    return x + _rms(y, w)


def _ffn_sublayer(x, mod_ref, npre_ref, npost_ref, idx, wgu_ref, wd_ref):
    h = _modulated_norm(x, npre_ref, mod_ref, idx)
    y = _swiglu(h.astype(BF16), wgu_ref, wd_ref)
    return _gated_residual(x, y, npost_ref, mod_ref, idx, 0.5)


def _layer_in_kernel(x_ref, mod_ref, npre_ref, npost_ref, wgu_ref, wd_ref, win_ref, *out_refs,
                     proj_cols, write_x):
    x1 = _ffn_sublayer(x_ref[0], mod_ref, npre_ref, npost_ref, 0, wgu_ref, wd_ref)
    outs = list(out_refs)
    if write_x:
        outs.pop(0)[0] = x1
    h = _modulated_norm(x1, npre_ref, mod_ref, 1).astype(BF16)
    for o_ref, (lo, width, mult) in zip(outs, proj_cols):
        p = _dot(h, win_ref[:, lo:lo + width])
        if mult != 1.0:
            p = p * mult
        o_ref[0] = p.astype(o_ref.dtype)


def _layer_in(x, mod, npre, npost, wgu, wd, win, proj, write_x):
    b, t, d = x.shape
    tm = min(TOKEN_TILE, t)
    assert t % tm == 0
    tok = lambda w: pl.BlockSpec((1, tm, w), lambda i, j: (i, j, 0))
    out_shape, out_specs = [], []
    if write_x:
        out_shape.append(jax.ShapeDtypeStruct((b, t, d), F32))
        out_specs.append(tok(d))
    for _, width, _, dtype in proj:
        out_shape.append(jax.ShapeDtypeStruct((b, t, width), dtype))
        out_specs.append(tok(width))
    return pl.pallas_call(
        functools.partial(_layer_in_kernel, proj_cols=tuple(p[:3] for p in proj), write_x=write_x),
        grid=(b, t // tm),
        in_specs=[
            tok(d),
            pl.BlockSpec((1, N_MOD, d), lambda i, j: (i, 0, 0)),
            _resident(npre.shape), _resident(npost.shape),
            _resident(wgu.shape), _resident(wd.shape), _resident(win.shape),
        ],
        out_specs=out_specs,
        out_shape=out_shape,
        compiler_params=pltpu.CompilerParams(
            dimension_semantics=("parallel", "parallel"), vmem_limit_bytes=VMEM_LIMIT_BYTES),
        name="layer_in",
    )(x, mod, npre, npost, wgu, wd, win)


def _layer_out_kernel(x_ref, yr_ref, ya_ref, mod_ref, npre_ref, npost_ref, wout_ref, wgu_ref, wd_ref,
                      o_ref):
    y = _dot(yr_ref[0], wout_ref[:D_REC, :]) + _dot(ya_ref[0], wout_ref[D_REC:, :])
    x2 = _gated_residual(x_ref[0], y, npost_ref, mod_ref, 1, 1.0)
    o_ref[0] = _ffn_sublayer(x2, mod_ref, npre_ref, npost_ref, 2, wgu_ref, wd_ref)


def _layer_out(x1, y_rec, y_att, mod, npre, npost, wout, wgu, wd):
    b, t, d = x1.shape
    tm = min(OUT_TOKEN_TILE, t)
    assert t % tm == 0
    tok = lambda w: pl.BlockSpec((1, tm, w), lambda i, j: (i, j, 0))
    return pl.pallas_call(
        _layer_out_kernel,
        grid=(b, t // tm),
        in_specs=[
            tok(d), tok(D_REC), tok(D_ATT),
            pl.BlockSpec((1, N_MOD, d), lambda i, j: (i, 0, 0)),
            _resident(npre.shape), _resident(npost.shape),
            _resident(wout.shape), _resident(wgu.shape), _resident(wd.shape),
        ],
        out_specs=tok(d),
        out_shape=jax.ShapeDtypeStruct((b, t, d), F32),
        compiler_params=pltpu.CompilerParams(
            dimension_semantics=("parallel", "parallel"), vmem_limit_bytes=VMEM_LIMIT_BYTES),
        name="layer_out",
    )(x1, y_rec, y_att, mod, npre, npost, wout, wgu, wd)


def _sublane_scan(a, u, row, reverse):
    for s in (1, 2, 4):
        if reverse:
            keep = row < SUBLANES - s
            shift = SUBLANES - s
        else:
            keep = row >= s
            shift = s
        a_sh = jnp.where(keep, pltpu.roll(a, shift, 0), 1.0)
        u_sh = jnp.where(keep, pltpu.roll(u, shift, 0), 0.0)
        u = a * u_sh + u
        a = a * a_sh
    return a, u


def _scan_block(a_ref, u_ref, h_ref, t0, carry, row, reverse):
    steps = range(SUBLANES - 1, -1, -1) if reverse else range(SUBLANES)
    h = p = None
    hs, ps = [None] * SUBLANES, [None] * SUBLANES
    for r in steps:
        a = a_ref[pl.ds(t0 + r, SUBLANES, stride=SUBLANES), :]
        u = u_ref[pl.ds(t0 + r, SUBLANES, stride=SUBLANES), :]
        h, p = (u, a) if h is None else (a * h + u, a * p)
        hs[r], ps[r] = h, p
    pp, hh = _sublane_scan(p, h, row, reverse)
    after = pp * carry + hh
    if reverse:
        before = jnp.where(row < SUBLANES - 1, pltpu.roll(after, SUBLANES - 1, 0), carry)
    else:
        before = jnp.where(row >= 1, pltpu.roll(after, 1, 0), carry)
    for r in range(SUBLANES):
        h_ref[pl.ds(t0 + r, SUBLANES, stride=SUBLANES), :] = hs[r] + ps[r] * before
    last = 0 if reverse else SUBLANES - 1
    return pp[last:last + 1, :] * carry + hh[last:last + 1, :]


def _rglru_kernel(xr_ref, xc_ref, gr_ref, cw_ref, cb_ref, wg_ref, bg_ref, lam_ref, o_ref,
                  pad_s, conv_s, af_s, uf_s, ab_s, ub_s, hf_s, hb_s):
    cg = REC_GROUP
    lam = lam_ref[0]
    log_sig_lam = jnp.minimum(lam, 0.0) - jnp.log(1.0 + jnp.exp(-jnp.abs(lam)))
    neg_half_c_lsl = (-0.5 * LRU_C) * log_sig_lam
    row = lax.broadcasted_iota(jnp.int32, (SUBLANES, cg), 0)

    def conv_gates(src_ref, n):
        halo = SUBLANES
        pad_s[0:halo, :] = jnp.zeros((halo, cg), F32)
        pad_s[halo:halo + n, :] = src_ref[0]
        pad_s[halo + n:2 * halo + n, :] = jnp.zeros((halo, cg), F32)
        y = cb_ref[...]
        for k in range(CONV_W):
            off = halo + k - CONV_PAD_LEFT
            y = y + pad_s[off:off + n, :] * cw_ref[k:k + 1, :]
        conv_s[0:n, :] = y

        chunk = min(GATE_CHUNK, n)

        def gate_body(ci, _):
            r0 = pl.multiple_of(ci * chunk, chunk)
            xv = conv_s[pl.ds(r0, chunk), :]
            g = _dot(xv.astype(BF16), wg_ref[0]) + bg_ref[0]
            x_half = 0.5 * xv
            for d, (a_s, u_s) in enumerate(((af_s, uf_s), (ab_s, ub_s))):
                ta = jnp.tanh(g[:, (2 * d) * cg:(2 * d + 1) * cg])
                tx = jnp.tanh(g[:, (2 * d + 1) * cg:(2 * d + 2) * cg])
                neg_log_a = neg_half_c_lsl[:, d * cg:(d + 1) * cg] * (ta + 1.0)
                a = jnp.exp2(neg_log_a * (-LOG2_E))
                a_s[pl.ds(r0, chunk), :] = a
                one_minus_a2 = jnp.tanh(neg_log_a) * (a * a + 1.0)
                root = jnp.where(one_minus_a2 > 0.0, one_minus_a2 * lax.rsqrt(one_minus_a2), 0.0)
                u_s[pl.ds(r0, chunk), :] = root * (tx + 1.0) * x_half
            return 0

        lax.fori_loop(0, n // chunk, gate_body, 0)

    def scan(n, hf0, hb0):
        blocks = n // SCAN_BLOCK

        def body(k, carry):
            hf, hb = carry
            tf = pl.multiple_of(k * SCAN_BLOCK, SCAN_BLOCK)
            tb = pl.multiple_of((blocks - 1 - k) * SCAN_BLOCK, SCAN_BLOCK)
            hf = _scan_block(af_s, uf_s, hf_s, tf, hf, row, False)
            hb = _scan_block(ab_s, ub_s, hb_s, tb, hb, row, True)
            return hf, hb

        return lax.fori_loop(0, blocks, body, (hf0, hb0), unroll=max(1, min(SCAN_UNROLL, blocks // 2)))

    n_ctx = xc_ref.shape[1]
    n_lat = xr_ref.shape[1]
    zero = jnp.zeros((1, cg), F32)
    conv_gates(xc_ref, n_ctx)
    hf, hb = scan(n_ctx, zero, zero)
    conv_gates(xr_ref, n_lat)
    scan(n_lat, hf, hb)
    y = hf_s[0:n_lat, :] + hb_s[0:n_lat, :]
    o_ref[0] = (y * jax.nn.gelu(gr_ref[0])).astype(o_ref.dtype)


def _rglru(xr, xr_ctx, gr, conv_w, conv_b, wg, bg, lam):
    b, t, _ = xr.shape
    n_ctx = xr_ctx.shape[1]
    cg = REC_GROUP
    seq = lambda n: pl.BlockSpec((1, n, cg), lambda i, j: (i, 0, j))
    grp = lambda shape: pl.BlockSpec((1,) + shape, lambda i, j: (j, 0, 0))
    return pl.pallas_call(
        _rglru_kernel,
        grid=(b, D_REC // cg),
        in_specs=[
            seq(t), seq(n_ctx), seq(t),
            pl.BlockSpec((CONV_W, cg), lambda i, j: (0, j)),
            pl.BlockSpec((1, cg), lambda i, j: (0, j)),
            grp((cg, 4 * cg)), grp((1, 4 * cg)), grp((1, 2 * cg)),
        ],
        out_specs=seq(t),
        out_shape=jax.ShapeDtypeStruct((b, t, D_REC), BF16),
        scratch_shapes=[pltpu.VMEM((t + 2 * SUBLANES, cg), F32)] + [pltpu.VMEM((t, cg), F32)] * 7,
        compiler_params=pltpu.CompilerParams(
            dimension_semantics=("parallel", "parallel"), vmem_limit_bytes=VMEM_LIMIT_BYTES),
        name="rglru",
    )(xr, xr_ctx, gr, conv_w, conv_b.reshape(1, D_REC), wg, bg, lam)


def _rglru_params(lru_wa, lru_ba, lru_wx, lru_bx, lru_lambda):
    cg = REC_GROUP
    n_grp = D_REC // cg
    per = cg // REC_BLOCK

    def block_diag(w):
        w = w.reshape(n_grp, per, REC_BLOCK, REC_BLOCK)
        eye = jnp.eye(per, dtype=w.dtype)
        return jnp.einsum("gpcd,pq->gpcqd", w, eye).reshape(n_grp, cg, cg)

    wg = (0.5 * jnp.concatenate([block_diag(lru_wa[0]), block_diag(lru_wx[0]),
                                 block_diag(lru_wa[1]), block_diag(lru_wx[1])], axis=-1)).astype(BF16)
    vec = lambda v: v.reshape(n_grp, 1, cg)
    bg = 0.5 * jnp.concatenate([vec(lru_ba[0]), vec(lru_bx[0]), vec(lru_ba[1]), vec(lru_bx[1])], axis=-1)
    lam = jnp.concatenate([vec(lru_lambda[0]), vec(lru_lambda[1])], axis=-1)
    return wg, bg, lam


def _attn_kernel(q_ref, k_ref, v_ref, kc_ref, vc_ref, tbl_ref, o_ref, s_buf, p_buf, *, rows):
    def block(j, _):
        _attn_block(pl.program_id(1) * ATT_STEP_BLOCKS + j, pl.multiple_of(j * ATT_Q_ROWS * GRID_W, GRID_W),
                    q_ref, k_ref, v_ref, kc_ref, vc_ref, tbl_ref, o_ref, s_buf, p_buf, rows)
        return 0

    lax.fori_loop(0, ATT_STEP_BLOCKS, block, 0)


def _attn_block(i, q_off, q_ref, k_ref, v_ref, kc_ref, vc_ref, tbl_ref, o_ref, s_buf, p_buf, rows):
    n_blk = rows // ATT_Q_ROWS
    start = jnp.clip(ATT_Q_ROWS * i - WIN_ROWS // 2, 0, rows - ATT_BAND_ROWS)
    kind = jnp.where(i == 0, 0, jnp.where(i == n_blk - 1, 2, 1))
    off = pl.multiple_of(start * GRID_W, GRID_W)
    band = ATT_BAND_ROWS * GRID_W
    tq = ATT_Q_ROWS * GRID_W
    lane = lax.broadcasted_iota(jnp.int32, (tq, LANES), 1)
    contract_last = (((1,), (1,)), ((), ()))

    def head_cols(head):
        pair = head // HEAD_PAIR
        return slice(pair * LANES, (pair + 1) * LANES)

    def scores(head):
        cols, hh, slot = head_cols(head), head % HEAD_PAIR, head % 2
        q = q_ref[0, pl.ds(q_off, tq), cols]
        mine = (lane >= hh * HEAD_DIM) & (lane < (hh + 1) * HEAD_DIM)
        qm = jnp.where(mine, q, jnp.zeros_like(q))
        s_loc = lax.dot_general(qm, k_ref[0, pl.ds(off, band), cols], contract_last,
                                preferred_element_type=F32)
        s_buf[slot, :, :band] = s_loc
        s_buf[slot, :, band:] = lax.dot_general(qm, kc_ref[0, :, cols], contract_last,
                                                preferred_element_type=F32)

    def softmax(head):
        slot = head % 2
        sums = []
        for r0 in range(0, tq, ATT_ROW_CHUNK):
            s = jnp.concatenate(
                [s_buf[slot, r0:r0 + ATT_ROW_CHUNK, :band] + tbl_ref[kind, head, r0:r0 + ATT_ROW_CHUNK, :],
                 s_buf[slot, r0:r0 + ATT_ROW_CHUNK, band:]], axis=-1)
            e =jnp.exp(s - jnp.max(s, axis=-1, keepdims=True))
            sums.append(jnp.sum(e, axis=-1, keepdims=True))
            p_buf[slot, r0:r0 + ATT_ROW_CHUNK, :] = e.astype(BF16)
        return jnp.concatenate(sums, axis=0)

    def values(head, denom):
        cols, slot = head_cols(head), head % 2
        o = (_dot(p_buf[slot, :, :band], v_ref[0, pl.ds(off, band), cols])
             + _dot(p_buf[slot, :, band:], vc_ref[0, :, cols]))
        return o / denom

    scores(0)
    outs = []
    for head in range(N_ATT_HEADS):
        if head + 1 < N_ATT_HEADS:
            scores(head + 1)
        outs.append(values(head, softmax(head)))
        if head % HEAD_PAIR == HEAD_PAIR - 1:
            pair_out = jnp.where(lane < HEAD_DIM, outs[-2], outs[-1])
            o_ref[0, pl.ds(q_off, tq), head_cols(head)] = pair_out.astype(o_ref.dtype)


def _attention(q, k, v, k_ctx, v_ctx, tbl):
    b, t, _ = q.shape
    n_ctx = k_ctx.shape[1]
    rows = t // GRID_W
    tq = ATT_Q_ROWS * GRID_W
    n_keys = ATT_BAND_ROWS * GRID_W + n_ctx
    full = lambda n: pl.BlockSpec((1, n, D_ATT), lambda bi, i: (bi, 0, 0))
    blk = pl.BlockSpec((1, ATT_STEP_BLOCKS * tq, D_ATT), lambda bi, i: (bi, i, 0))
    return pl.pallas_call(
        functools.partial(_attn_kernel, rows=rows),
        grid=(b, rows // (ATT_Q_ROWS * ATT_STEP_BLOCKS)),
        in_specs=[blk, full(t), full(t), full(n_ctx), full(n_ctx), _resident(tbl.shape)],
        out_specs=blk,
        out_shape=jax.ShapeDtypeStruct((b, t, D_ATT), BF16),
        scratch_shapes=[pltpu.VMEM((2, tq, n_keys), F32), pltpu.VMEM((2, tq, n_keys), BF16)],
        compiler_params=pltpu.CompilerParams(
            dimension_semantics=("parallel", "arbitrary"), vmem_limit_bytes=VMEM_LIMIT_BYTES),
        name="attn",
    )(q, k, v, k_ctx, v_ctx, tbl)


def _bias_tables(rpb, rows):
    n_blk = rows // ATT_Q_ROWS
    n_off = 2 * WIN_ROWS - 1
    pad = ATT_BAND_ROWS
    n_pairs = n_off + 2 * pad
    col = np.arange(GRID_W)
    col_off = np.clip(col[None, :] - col[:, None] + WIN_COLS - 1, 0, 2 * WIN_COLS - 2)
    n_col_off = 2 * WIN_COLS - 1
    rp = jnp.pad(rpb, ((0, 0), (pad, pad + 1), (0, 0)))
    z = jnp.concatenate([rp[:, :n_pairs], rp[:, 1:n_pairs + 1]], axis=-1)
    m = np.arange(2 * GRID_W)
    f = np.arange(2 * n_col_off)
    select = ((f[:, None, None] // n_col_off == m[None, None, :] // GRID_W)
              & (f[:, None, None] % n_col_off == col_off[:, m % GRID_W][None])).astype(np.float32)
    pairs = jnp.einsum("hRf,fqm->hRqm", z, select, precision=lax.Precision.HIGHEST)

    plan = []
    for i in (0, 1, n_blk - 1):
        band_start = int(np.clip(ATT_Q_ROWS * i - WIN_ROWS // 2, 0, rows - ATT_BAND_ROWS))
        per_row = []
        for rq in range(ATT_Q_ROWS):
            r = ATT_Q_ROWS * i + rq
            kr = band_start + np.arange(ATT_BAND_ROWS)
            win_start = int(np.clip(r - WIN_ROWS // 2, 0, rows - WIN_ROWS))
            row_valid = tuple(bool(ok) for ok in (kr >= win_start) & (kr < win_start + WIN_ROWS))
            first_off = band_start - r + WIN_ROWS - 1
            assert -pad <= first_off <= n_off
            per_row.append((first_off + pad, row_valid))
        plan.append(tuple(per_row))

    tq, band = ATT_Q_ROWS * GRID_W, ATT_BAND_ROWS * GRID_W
    return pl.pallas_call(
        functools.partial(_table_kernel, plan=tuple(plan)),
        grid=(N_ATT_HEADS,),
        in_specs=[pl.BlockSpec((1, n_pairs, GRID_W, 2 * GRID_W), lambda h: (h, 0, 0, 0))],
        out_specs=pl.BlockSpec((3, 1, tq, band), lambda h: (0, h, 0, 0)),
        out_shape=jax.ShapeDtypeStruct((3, N_ATT_HEADS, tq, band), F32),
        name="bias_table",
    )(pairs)


def _table_kernel(pairs_ref, o_ref, *, plan):
    qc = lax.broadcasted_iota(jnp.int32, (GRID_W, 2 * GRID_W), 0)
    m = lax.broadcasted_iota(jnp.int32, (GRID_W, 2 * GRID_W), 1)
    kc = jnp.where(m < GRID_W, m, m - GRID_W)
    col_start = jnp.clip(qc - WIN_COLS // 2, 0, GRID_W - WIN_COLS)
    col_ok = (kc >= col_start) & (kc < col_start + WIN_COLS)
    first_half = m < GRID_W
    masked = jnp.full((GRID_W, 2 * GRID_W), NEG_INF, F32)
    for kind, per_row in enumerate(plan):
        for rq, (lo, row_valid) in enumerate(per_row):
            for jp in range(ATT_BAND_ROWS // 2):
                ok0, ok1 = row_valid[2 * jp], row_valid[2 * jp + 1]
                if ok0 and ok1:
                    ok = col_ok
                elif ok0:
                    ok = col_ok & first_half
                elif ok1:
                    ok = col_ok & jnp.logical_not(first_half)
                else:
                    ok = None
                tile = masked if ok is None else jnp.where(ok, pairs_ref[0, lo + 2 * jp], NEG_INF)
                o_ref[kind, 0, rq * GRID_W:(rq + 1) * GRID_W, jp * 2 * GRID_W:(jp + 1) * 2 * GRID_W] = tile


def kernel(x, c, ctx, c_ctx, w_mod, b_mod, norm_pre, norm_post, ffn1_w_gu, ffn1_w_down, ffn2_w_gu,
           ffn2_w_down, w_in, w_out, conv_w, conv_b, lru_wa, lru_ba, lru_wx, lru_bx, lru_lambda, na_rpb):
    assert w_mod.shape[0] == 1, "single-layer problem"
    b, t, d = x.shape
    rows = t // GRID_W
    assert d == D_MODEL and t % (ATT_STEP_BLOCKS * ATT_Q_ROWS * GRID_W) == 0 and rows >= 3 * ATT_Q_ROWS
    assert b + 1 <= MOD_ROWS
    for n in (t, ctx.shape[1]):
        assert n % SCAN_BLOCK == 0 and n % min(GATE_CHUNK, n) == 0

    c_rows = jnp.concatenate([c, c_ctx[None], jnp.zeros((MOD_ROWS - b - 1, d), F32)], axis=0)
    mod_all = _modulation(c_rows, w_mod[0], b_mod[0])
    mod = mod_all[:b].reshape(b, N_MOD, d)
    mod_ctx = mod_all[b].reshape(1, N_MOD, d)

    npre, npost = norm_pre[0], norm_post[0]
    wgu1, wd1 = ffn1_w_gu[0].astype(BF16), ffn1_w_down[0].astype(BF16)
    wgu2, wd2 = ffn2_w_gu[0].astype(BF16), ffn2_w_down[0].astype(BF16)
    win, wout = w_in[0].astype(BF16), w_out[0].astype(BF16)

    q_scale = HEAD_DIM ** -0.5
    x1, xr, gr, q, k, v = _layer_in(
        x, mod, npre, npost, wgu1, wd1, win,
        proj=((0, D_REC, 1.0, F32), (C_RG, D_REC, 1.0, F32), (C_Q, D_ATT, q_scale, BF16),
              (C_K, D_ATT, 1.0, BF16), (C_V, D_ATT, 1.0, BF16)),
        write_x=True)
    n_ctx = ctx.shape[1]
    ctx_outs = _layer_in(
        ctx.reshape(1, b * n_ctx, d), mod_ctx, npre, npost, wgu1, wd1, win,
        proj=((0, D_REC, 1.0, F32), (C_K, D_ATT, 1.0, BF16), (C_V, D_ATT, 1.0, BF16)),
        write_x=False)
    xr_ctx, k_ctx, v_ctx = (o.reshape(b, n_ctx, o.shape[-1]) for o in ctx_outs)

    wg, bg, lam = _rglru_params(lru_wa[0], lru_ba[0], lru_wx[0], lru_bx[0], lru_lambda[0])
    y_rec = _rglru(xr, xr_ctx, gr, conv_w[0], conv_b[0], wg, bg, lam)
    y_att = _attention(q, k, v, k_ctx, v_ctx, _bias_tables(na_rpb[0].astype(F32), rows))
    return _layer_out(x1, y_rec, y_att, mod, npre, npost, wout, wgu2, wd2)
```

```python
import functools

import numpy as np
import jax
import jax.numpy as jnp
from jax import lax
from jax.experimental import pallas as pl
from jax.experimental.pallas import tpu as pltpu

F32 = jnp.float32
BF16 = jnp.bfloat16

D_MODEL = 1024
D_REC = 512
D_ATT = 512
HEAD_DIM = 64
N_ATT_HEADS = 8
N_REC_BLOCKS = 8
REC_BLOCK = 64
CONV_W = 4
CONV_PAD_LEFT = 2
LRU_C = 8.0
GRID_W = 64
WIN_ROWS = 8
WIN_COLS = 16
D_FF = 2816
N_MOD = 9
EPS = 1e-6
NEG_INF = -1e30
LOG2_E = 1.4426950408889634
C_RG, C_Q, C_K, C_V = 512, 1024, 1536, 2048
D_IN = 2560

LANES = 128
SUBLANES = 8
VMEM_LIMIT_BYTES = 56 * 1024 * 1024

TOKEN_TILE = 512
OUT_TOKEN_TILE = 1024
FF_CHUNK = 256
MOD_ROWS = 16
REC_GROUP = LANES
GATE_CHUNK = 2048
SCAN_BLOCK = 64
SCAN_UNROLL = 8
ATT_Q_ROWS = 4
ATT_BAND_ROWS = 12
ATT_STEP_BLOCKS = 8
HEAD_PAIR = LANES // HEAD_DIM
ATT_ROW_CHUNK = 32


def _sigmoid(x):
    return 0.5 * (jnp.tanh(0.5 * x) + 1.0)


def _rms(x, g):
    return x * lax.rsqrt(jnp.mean(x * x, axis=-1, keepdims=True) + EPS) * g


def _dot(a, b):
    return jnp.dot(a, b, preferred_element_type=F32)


def _resident(shape):
    return pl.BlockSpec(shape, lambda *_: (0,) * len(shape), pipeline_mode=pl.Buffered(1))


def _mod_kernel(c_ref, w_ref, b_ref, o_ref):
    c = c_ref[...]
    s = (c * _sigmoid(c)).astype(BF16)
    o_ref[...] = _dot(s, w_ref[...].astype(BF16)) + b_ref[...]


def _modulation(c_rows, w_mod, b_mod):
    n = w_mod.shape[1]
    return pl.pallas_call(
        _mod_kernel,
        grid=(n // D_MODEL,),
        in_specs=[
            pl.BlockSpec((MOD_ROWS, D_MODEL), lambda j: (0, 0)),
            pl.BlockSpec((D_MODEL, D_MODEL), lambda j: (0, j)),
            pl.BlockSpec((1, D_MODEL), lambda j: (0, j)),
        ],
        out_specs=pl.BlockSpec((MOD_ROWS, D_MODEL), lambda j: (0, j)),
        out_shape=jax.ShapeDtypeStruct((MOD_ROWS, n), F32),
        name="mod",
    )(c_rows, w_mod, b_mod.reshape(1, n))


def _swiglu(h, wgu_ref, wd_ref):
    acc = None
    for c in range(D_FF // FF_CHUNK):
        lo = c * FF_CHUNK
        g = _dot(h, wgu_ref[:, lo:lo + FF_CHUNK])
        u = _dot(h, wgu_ref[:, D_FF + lo:D_FF + lo + FF_CHUNK])
        half = 0.5 * g
        a = (half * (jnp.tanh(half) + 1.0) * u).astype(BF16)
        d = _dot(a, wd_ref[lo:lo + FF_CHUNK, :])
        acc = d if acc is None else acc + d
    return acc


def _modulated_norm(x, npre_ref, mod_ref, idx):
    w = npre_ref[idx:idx + 1, :] * (1.0 + mod_ref[0, 3 * idx + 1:3 * idx + 2, :])
    return _rms(x, w) + mod_ref[0, 3 * idx:3 * idx + 1, :]


def _gated_residual(x, y, npost_ref, mod_ref, idx, res_w):
    w = (res_w * mod_ref[0, 3 * idx + 2:3 * idx + 3, :]) * npost_ref[idx:idx + 1, :]
    return x + _rms(y, w)


def _ffn_sublayer(x, mod_ref, npre_ref, npost_ref, idx, wgu_ref, wd_ref):
    h = _modulated_norm(x, npre_ref, mod_ref, idx)
    y = _swiglu(h.astype(BF16), wgu_ref, wd_ref)
    return _gated_residual(x, y, npost_ref, mod_ref, idx, 0.5)


def _layer_in_kernel(x_ref, mod_ref, npre_ref, npost_ref, wgu_ref, wd_ref, win_ref, *out_refs,
                     proj_cols, write_x):
    x1 = _ffn_sublayer(x_ref[0], mod_ref, npre_ref, npost_ref, 0, wgu_ref, wd_ref)
    outs = list(out_refs)
    if write_x:
        outs.pop(0)[0] = x1
    h = _modulated_norm(x1, npre_ref, mod_ref, 1).astype(BF16)
    for o_ref, (lo, width, mult) in zip(outs, proj_cols):
        p = _dot(h, win_ref[:, lo:lo + width])
        if mult != 1.0:
            p = p * mult
        o_ref[0] = p.astype(o_ref.dtype)


def _layer_in(x, mod, npre, npost, wgu, wd, win, proj, write_x):
    b, t, d = x.shape
    tm = min(TOKEN_TILE, t)
    assert t % tm == 0
    tok = lambda w: pl.BlockSpec((1, tm, w), lambda i, j: (i, j, 0))
    out_shape, out_specs = [], []
    if write_x:
        out_shape.append(jax.ShapeDtypeStruct((b, t, d), F32))
        out_specs.append(tok(d))
    for _, width, _, dtype in proj:
        out_shape.append(jax.ShapeDtypeStruct((b, t, width), dtype))
        out_specs.append(tok(width))
    return pl.pallas_call(
        functools.partial(_layer_in_kernel, proj_cols=tuple(p[:3] for p in proj), write_x=write_x),
        grid=(b, t // tm),
        in_specs=[
            tok(d),
            pl.BlockSpec((1, N_MOD, d), lambda i, j: (i, 0, 0)),
            _resident(npre.shape), _resident(npost.shape),
            _resident(wgu.shape), _resident(wd.shape), _resident(win.shape),
        ],
        out_specs=out_specs,
        out_shape=out_shape,
        compiler_params=pltpu.CompilerParams(
            dimension_semantics=("parallel", "parallel"), vmem_limit_bytes=VMEM_LIMIT_BYTES),
        name="layer_in",
    )(x, mod, npre, npost, wgu, wd, win)


def _layer_out_kernel(x_ref, yr_ref, ya_ref, mod_ref, npre_ref, npost_ref, wout_ref, wgu_ref, wd_ref,
                      o_ref):
    y = _dot(yr_ref[0], wout_ref[:D_REC, :]) + _dot(ya_ref[0], wout_ref[D_REC:, :])
    x2 = _gated_residual(x_ref[0], y, npost_ref, mod_ref, 1, 1.0)
    o_ref[0] = _ffn_sublayer(x2, mod_ref, npre_ref, npost_ref, 2, wgu_ref, wd_ref)


def _layer_out(x1, y_rec, y_att, mod, npre, npost, wout, wgu, wd):
    b, t, d = x1.shape
    tm = min(OUT_TOKEN_TILE, t)
    assert t % tm == 0
    tok = lambda w: pl.BlockSpec((1, tm, w), lambda i, j: (i, j, 0))
    return pl.pallas_call(
        _layer_out_kernel,
        grid=(b, t // tm),
        in_specs=[
            tok(d), tok(D_REC), tok(D_ATT),
            pl.BlockSpec((1, N_MOD, d), lambda i, j: (i, 0, 0)),
            _resident(npre.shape), _resident(npost.shape),
            _resident(wout.shape), _resident(wgu.shape), _resident(wd.shape),
        ],
        out_specs=tok(d),
        out_shape=jax.ShapeDtypeStruct((b, t, d), F32),
        compiler_params=pltpu.CompilerParams(
            dimension_semantics=("parallel", "parallel"), vmem_limit_bytes=VMEM_LIMIT_BYTES),
        name="layer_out",
    )(x1, y_rec, y_att, mod, npre, npost, wout, wgu, wd)


def _sublane_scan(a, u, row, reverse):
    for s in (1, 2, 4):
        if reverse:
            keep = row < SUBLANES - s
            shift = SUBLANES - s
        else:
            keep = row >= s
            shift = s
        a_sh = jnp.where(keep, pltpu.roll(a, shift, 0), 1.0)
        u_sh = jnp.where(keep, pltpu.roll(u, shift, 0), 0.0)
        u = a * u_sh + u
        a = a * a_sh
    return a, u


def _scan_block(a_ref, u_ref, h_ref, t0, carry, row, reverse):
    steps = range(SUBLANES - 1, -1, -1) if reverse else range(SUBLANES)
    h = p = None
    hs, ps = [None] * SUBLANES, [None] * SUBLANES
    for r in steps:
        a = a_ref[pl.ds(t0 + r, SUBLANES, stride=SUBLANES), :]
        u = u_ref[pl.ds(t0 + r, SUBLANES, stride=SUBLANES), :]
        h, p = (u, a) if h is None else (a * h + u, a * p)
        hs[r], ps[r] = h, p
    pp, hh = _sublane_scan(p, h, row, reverse)
    after = pp * carry + hh
    if reverse:
        before = jnp.where(row < SUBLANES - 1, pltpu.roll(after, SUBLANES - 1, 0), carry)
    else:
        before = jnp.where(row >= 1, pltpu.roll(after, 1, 0), carry)
    for r in range(SUBLANES):
        h_ref[pl.ds(t0 + r, SUBLANES, stride=SUBLANES), :] = hs[r] + ps[r] * before
    last = 0 if reverse else SUBLANES - 1
    return pp[last:last + 1, :] * carry + hh[last:last + 1, :]


def _rglru_kernel(xr_ref, xc_ref, gr_ref, cw_ref, cb_ref, wg_ref, bg_ref, lam_ref, o_ref,
                  pad_s, conv_s, af_s, uf_s, ab_s, ub_s, hf_s, hb_s):
    cg = REC_GROUP
    lam = lam_ref[0]
    log_sig_lam = jnp.minimum(lam, 0.0) - jnp.log(1.0 + jnp.exp(-jnp.abs(lam)))
    neg_half_c_lsl = (-0.5 * LRU_C) * log_sig_lam
    row = lax.broadcasted_iota(jnp.int32, (SUBLANES, cg), 0)

    def conv_gates(src_ref, n):
        halo = SUBLANES
        pad_s[0:halo, :] = jnp.zeros((halo, cg), F32)
        pad_s[halo:halo + n, :] = src_ref[0]
        pad_s[halo + n:2 * halo + n, :] = jnp.zeros((halo, cg), F32)
        y = cb_ref[...]
        for k in range(CONV_W):
            off = halo + k - CONV_PAD_LEFT
            y = y + pad_s[off:off + n, :] * cw_ref[k:k + 1, :]
        conv_s[0:n, :] = y

        chunk = min(GATE_CHUNK, n)

        def gate_body(ci, _):
            r0 = pl.multiple_of(ci * chunk, chunk)
            xv = conv_s[pl.ds(r0, chunk), :]
            g = _dot(xv.astype(BF16), wg_ref[0]) + bg_ref[0]
            x_half = 0.5 * xv
            for d, (a_s, u_s) in enumerate(((af_s, uf_s), (ab_s, ub_s))):
                ta = jnp.tanh(g[:, (2 * d) * cg:(2 * d + 1) * cg])
                tx = jnp.tanh(g[:, (2 * d + 1) * cg:(2 * d + 2) * cg])
                neg_log_a = neg_half_c_lsl[:, d * cg:(d + 1) * cg] * (ta + 1.0)
                a = jnp.exp2(neg_log_a * (-LOG2_E))
                a_s[pl.ds(r0, chunk), :] = a
                one_minus_a2 = jnp.tanh(neg_log_a) * (a * a + 1.0)
                root = jnp.where(one_minus_a2 > 0.0, one_minus_a2 * lax.rsqrt(one_minus_a2), 0.0)
                u_s[pl.ds(r0, chunk), :] = root * (tx + 1.0) * x_half
            return 0

        lax.fori_loop(0, n // chunk, gate_body, 0)

    def scan(n, hf0, hb0):
        blocks = n // SCAN_BLOCK

        def body(k, carry):
            hf, hb = carry
            tf = pl.multiple_of(k * SCAN_BLOCK, SCAN_BLOCK)
            tb = pl.multiple_of((blocks - 1 - k) * SCAN_BLOCK, SCAN_BLOCK)
            hf = _scan_block(af_s, uf_s, hf_s, tf, hf, row, False)
            hb = _scan_block(ab_s, ub_s, hb_s, tb, hb, row, True)
            return hf, hb

        return lax.fori_loop(0, blocks, body, (hf0, hb0), unroll=max(1, min(SCAN_UNROLL, blocks // 2)))

    n_ctx = xc_ref.shape[1]
    n_lat = xr_ref.shape[1]
    zero = jnp.zeros((1, cg), F32)
    conv_gates(xc_ref, n_ctx)
    hf, hb = scan(n_ctx, zero, zero)
    conv_gates(xr_ref, n_lat)
    scan(n_lat, hf, hb)
    y = hf_s[0:n_lat, :] + hb_s[0:n_lat, :]
    o_ref[0] = (y * jax.nn.gelu(gr_ref[0])).astype(o_ref.dtype)


def _rglru(xr, xr_ctx, gr, conv_w, conv_b, wg, bg, lam):
    b, t, _ = xr.shape
    n_ctx = xr_ctx.shape[1]
    cg = REC_GROUP
    seq = lambda n: pl.BlockSpec((1, n, cg), lambda i, j: (i, 0, j))
    grp = lambda shape: pl.BlockSpec((1,) + shape, lambda i, j: (j, 0, 0))
    return pl.pallas_call(
        _rglru_kernel,
        grid=(b, D_REC // cg),
        in_specs=[
            seq(t), seq(n_ctx), seq(t),
            pl.BlockSpec((CONV_W, cg), lambda i, j: (0, j)),
            pl.BlockSpec((1, cg), lambda i, j: (0, j)),
            grp((cg, 4 * cg)), grp((1, 4 * cg)), grp((1, 2 * cg)),
        ],
        out_specs=seq(t),
        out_shape=jax.ShapeDtypeStruct((b, t, D_REC), BF16),
        scratch_shapes=[pltpu.VMEM((t + 2 * SUBLANES, cg), F32)] + [pltpu.VMEM((t, cg), F32)] * 7,
        compiler_params=pltpu.CompilerParams(
            dimension_semantics=("parallel", "parallel"), vmem_limit_bytes=VMEM_LIMIT_BYTES),
        name="rglru",
    )(xr, xr_ctx, gr, conv_w, conv_b.reshape(1, D_REC), wg, bg, lam)


def _rglru_params(lru_wa, lru_ba, lru_wx, lru_bx, lru_lambda):
    cg = REC_GROUP
    n_grp = D_REC // cg
    per = cg // REC_BLOCK

    def block_diag(w):
        w = w.reshape(n_grp, per, REC_BLOCK, REC_BLOCK)
        eye = jnp.eye(per, dtype=w.dtype)
        return jnp.einsum("gpcd,pq->gpcqd", w, eye).reshape(n_grp, cg, cg)

    wg = (0.5 * jnp.concatenate([block_diag(lru_wa[0]), block_diag(lru_wx[0]),
                                 block_diag(lru_wa[1]), block_diag(lru_wx[1])], axis=-1)).astype(BF16)
    vec = lambda v: v.reshape(n_grp, 1, cg)
    bg = 0.5 * jnp.concatenate([vec(lru_ba[0]), vec(lru_bx[0]), vec(lru_ba[1]), vec(lru_bx[1])], axis=-1)
    lam = jnp.concatenate([vec(lru_lambda[0]), vec(lru_lambda[1])], axis=-1)
    return wg, bg, lam


def _attn_kernel(q_ref, k_ref, v_ref, kc_ref, vc_ref, tbl_ref, o_ref, s_buf, p_buf, *, rows):
    def block(j, _):
        _attn_block(pl.program_id(1) * ATT_STEP_BLOCKS + j, pl.multiple_of(j * ATT_Q_ROWS * GRID_W, GRID_W),
                    q_ref, k_ref, v_ref, kc_ref, vc_ref, tbl_ref, o_ref, s_buf, p_buf, rows)
        return 0

    lax.fori_loop(0, ATT_STEP_BLOCKS, block, 0)


def _attn_block(i, q_off, q_ref, k_ref, v_ref, kc_ref, vc_ref, tbl_ref, o_ref, s_buf, p_buf, rows):
    n_blk = rows // ATT_Q_ROWS
    start = jnp.clip(ATT_Q_ROWS * i - WIN_ROWS // 2, 0, rows - ATT_BAND_ROWS)
    kind = jnp.where(i == 0, 0, jnp.where(i == n_blk - 1, 2, 1))
    off = pl.multiple_of(start * GRID_W, GRID_W)
    band = ATT_BAND_ROWS * GRID_W
    tq = ATT_Q_ROWS * GRID_W
    lane = lax.broadcasted_iota(jnp.int32, (tq, LANES), 1)
    contract_last = (((1,), (1,)), ((), ()))

    def head_cols(head):
        pair = head // HEAD_PAIR
        return slice(pair * LANES, (pair + 1) * LANES)

    def scores(head):
        cols, hh, slot = head_cols(head), head % HEAD_PAIR, head % 2
        q = q_ref[0, pl.ds(q_off, tq), cols]
        mine = (lane >= hh * HEAD_DIM) & (lane < (hh + 1) * HEAD_DIM)
        qm = jnp.where(mine, q, jnp.zeros_like(q))
        s_loc = lax.dot_general(qm, k_ref[0, pl.ds(off, band), cols], contract_last,
                                preferred_element_type=F32)
        s_buf[slot, :, :band] = s_loc
        s_buf[slot, :, band:] = lax.dot_general(qm, kc_ref[0, :, cols], contract_last,
                                                preferred_element_type=F32)

    def softmax(head):
        slot = head % 2
        sums = []
        for r0 in range(0, tq, ATT_ROW_CHUNK):
            s = jnp.concatenate(
                [s_buf[slot, r0:r0 + ATT_ROW_CHUNK, :band] + tbl_ref[kind, head, r0:r0 + ATT_ROW_CHUNK, :],
                 s_buf[slot, r0:r0 + ATT_ROW_CHUNK, band:]], axis=-1)
            e =jnp.exp(s - jnp.max(s, axis=-1, keepdims=True))
            sums.append(jnp.sum(e, axis=-1, keepdims=True))
            p_buf[slot, r0:r0 + ATT_ROW_CHUNK, :] = e.astype(BF16)
        return jnp.concatenate(sums, axis=0)

    def values(head, denom):
        cols, slot = head_cols(head), head % 2
        o = (_dot(p_buf[slot, :, :band], v_ref[0, pl.ds(off, band), cols])
             + _dot(p_buf[slot, :, band:], vc_ref[0, :, cols]))
        return o / denom

    scores(0)
    outs = []
    for head in range(N_ATT_HEADS):
        if head + 1 < N_ATT_HEADS:
            scores(head + 1)
        outs.append(values(head, softmax(head)))
        if head % HEAD_PAIR == HEAD_PAIR - 1:
            pair_out = jnp.where(lane < HEAD_DIM, outs[-2], outs[-1])
            o_ref[0, pl.ds(q_off, tq), head_cols(head)] = pair_out.astype(o_ref.dtype)


def _attention(q, k, v, k_ctx, v_ctx, tbl):
    b, t, _ = q.shape
    n_ctx = k_ctx.shape[1]
    rows = t // GRID_W
    tq = ATT_Q_ROWS * GRID_W
    n_keys = ATT_BAND_ROWS * GRID_W + n_ctx
    full = lambda n: pl.BlockSpec((1, n, D_ATT), lambda bi, i: (bi, 0, 0))
    blk = pl.BlockSpec((1, ATT_STEP_BLOCKS * tq, D_ATT), lambda bi, i: (bi, i, 0))
    return pl.pallas_call(
        functools.partial(_attn_kernel, rows=rows),
        grid=(b, rows // (ATT_Q_ROWS * ATT_STEP_BLOCKS)),
        in_specs=[blk, full(t), full(t), full(n_ctx), full(n_ctx), _resident(tbl.shape)],
        out_specs=blk,
        out_shape=jax.ShapeDtypeStruct((b, t, D_ATT), BF16),
        scratch_shapes=[pltpu.VMEM((2, tq, n_keys), F32), pltpu.VMEM((2, tq, n_keys), BF16)],
        compiler_params=pltpu.CompilerParams(
            dimension_semantics=("parallel", "arbitrary"), vmem_limit_bytes=VMEM_LIMIT_BYTES),
        name="attn",
    )(q, k, v, k_ctx, v_ctx, tbl)


def _bias_tables(rpb, rows):
    n_blk = rows // ATT_Q_ROWS
    n_off = 2 * WIN_ROWS - 1
    pad = ATT_BAND_ROWS
    n_pairs = n_off + 2 * pad
    col = np.arange(GRID_W)
    col_off = np.clip(col[None, :] - col[:, None] + WIN_COLS - 1, 0, 2 * WIN_COLS - 2)
    n_col_off = 2 * WIN_COLS - 1
    rp = jnp.pad(rpb, ((0, 0), (pad, pad + 1), (0, 0)))
    z = jnp.concatenate([rp[:, :n_pairs], rp[:, 1:n_pairs + 1]], axis=-1)
    m = np.arange(2 * GRID_W)
    f = np.arange(2 * n_col_off)
    select = ((f[:, None, None] // n_col_off == m[None, None, :] // GRID_W)
              & (f[:, None, None] % n_col_off == col_off[:, m % GRID_W][None])).astype(np.float32)
    pairs = jnp.einsum("hRf,fqm->hRqm", z, select, precision=lax.Precision.HIGHEST)

    plan = []
    for i in (0, 1, n_blk - 1):
        band_start = int(np.clip(ATT_Q_ROWS * i - WIN_ROWS // 2, 0, rows - ATT_BAND_ROWS))
        per_row = []
        for rq in range(ATT_Q_ROWS):
            r = ATT_Q_ROWS * i + rq
            kr = band_start + np.arange(ATT_BAND_ROWS)
            win_start = int(np.clip(r - WIN_ROWS // 2, 0, rows - WIN_ROWS))
            row_valid = tuple(bool(ok) for ok in (kr >= win_start) & (kr < win_start + WIN_ROWS))
            first_off = band_start - r + WIN_ROWS - 1
            assert -pad <= first_off <= n_off
            per_row.append((first_off + pad, row_valid))
        plan.append(tuple(per_row))

    tq, band = ATT_Q_ROWS * GRID_W, ATT_BAND_ROWS * GRID_W
    return pl.pallas_call(
        functools.partial(_table_kernel, plan=tuple(plan)),
        grid=(N_ATT_HEADS,),
        in_specs=[pl.BlockSpec((1, n_pairs, GRID_W, 2 * GRID_W), lambda h: (h, 0, 0, 0))],
        out_specs=pl.BlockSpec((3, 1, tq, band), lambda h: (0, h, 0, 0)),
        out_shape=jax.ShapeDtypeStruct((3, N_ATT_HEADS, tq, band), F32),
        name="bias_table",
    )(pairs)


def _table_kernel(pairs_ref, o_ref, *, plan):
    qc = lax.broadcasted_iota(jnp.int32, (GRID_W, 2 * GRID_W), 0)
    m = lax.broadcasted_iota(jnp.int32, (GRID_W, 2 * GRID_W), 1)
    kc = jnp.where(m < GRID_W, m, m - GRID_W)
    col_start = jnp.clip(qc - WIN_COLS // 2, 0, GRID_W - WIN_COLS)
    col_ok = (kc >= col_start) & (kc < col_start + WIN_COLS)
    first_half = m < GRID_W
    masked = jnp.full((GRID_W, 2 * GRID_W), NEG_INF, F32)
    for kind, per_row in enumerate(plan):
        for rq, (lo, row_valid) in enumerate(per_row):
            for jp in range(ATT_BAND_ROWS // 2):
                ok0, ok1 = row_valid[2 * jp], row_valid[2 * jp + 1]
                if ok0 and ok1:
                    ok = col_ok
                elif ok0:
                    ok = col_ok & first_half
                elif ok1:
                    ok = col_ok & jnp.logical_not(first_half)
                else:
                    ok = None
                tile = masked if ok is None else jnp.where(ok, pairs_ref[0, lo + 2 * jp], NEG_INF)
                o_ref[kind, 0, rq * GRID_W:(rq + 1) * GRID_W, jp * 2 * GRID_W:(jp + 1) * 2 * GRID_W] = tile


def kernel(x, c, ctx, c_ctx, w_mod, b_mod, norm_pre, norm_post, ffn1_w_gu, ffn1_w_down, ffn2_w_gu,
           ffn2_w_down, w_in, w_out, conv_w, conv_b, lru_wa, lru_ba, lru_wx, lru_bx, lru_lambda, na_rpb):
    assert w_mod.shape[0] == 1, "single-layer problem"
    b, t, d = x.shape
    rows = t // GRID_W
    assert d == D_MODEL and t % (ATT_STEP_BLOCKS * ATT_Q_ROWS * GRID_W) == 0 and rows >= 3 * ATT_Q_ROWS
    assert b + 1 <= MOD_ROWS
    for n in (t, ctx.shape[1]):
        assert n % SCAN_BLOCK == 0 and n % min(GATE_CHUNK, n) == 0

    c_rows = jnp.concatenate([c, c_ctx[None], jnp.zeros((MOD_ROWS - b - 1, d), F32)], axis=0)
    mod_all = _modulation(c_rows, w_mod[0], b_mod[0])
    mod = mod_all[:b].reshape(b, N_MOD, d)
    mod_ctx = mod_all[b].reshape(1, N_MOD, d)

    npre, npost = norm_pre[0], norm_post[0]
    wgu1, wd1 = ffn1_w_gu[0].astype(BF16), ffn1_w_down[0].astype(BF16)
    wgu2, wd2 = ffn2_w_gu[0].astype(BF16), ffn2_w_down[0].astype(BF16)
    win, wout = w_in[0].astype(BF16), w_out[0].astype(BF16)

    q_scale = HEAD_DIM ** -0.5
    x1, xr, gr, q, k, v = _layer_in(
        x, mod, npre, npost, wgu1, wd1, win,
        proj=((0, D_REC, 1.0, F32), (C_RG, D_REC, 1.0, F32), (C_Q, D_ATT, q_scale, BF16),
              (C_K, D_ATT, 1.0, BF16), (C_V, D_ATT, 1.0, BF16)),
        write_x=True)
    n_ctx = ctx.shape[1]
    ctx_outs = _layer_in(
        ctx.reshape(1, b * n_ctx, d), mod_ctx, npre, npost, wgu1, wd1, win,
        proj=((0, D_REC, 1.0, F32), (C_K, D_ATT, 1.0, BF16), (C_V, D_ATT, 1.0, BF16)),
        write_x=False)
    xr_ctx, k_ctx, v_ctx = (o.reshape(b, n_ctx, o.shape[-1]) for o in ctx_outs)

    wg, bg, lam = _rglru_params(lru_wa[0], lru_ba[0], lru_wx[0], lru_bx[0], lru_lambda[0])
    y_rec = _rglru(xr, xr_ctx, gr, conv_w[0], conv_b[0], wg, bg, lam)
    y_att = _attention(q, k, v, k_ctx, v_ctx, _bias_tables(na_rpb[0].astype(F32), rows))
    return _layer_out(x1, y_rec, y_att, mod, npre, npost, wout, wgu2, wd2)
```

```python
import functools

import numpy as np
import jax
import jax.numpy as jnp
from jax import lax
from jax.experimental import pallas as pl
from jax.experimental.pallas import tpu as pltpu

F32 = jnp.float32
BF16 = jnp.bfloat16

D_MODEL = 1024
D_REC = 512
D_ATT = 512
HEAD_DIM = 64
N_ATT_HEADS = 8
N_REC_BLOCKS = 8
REC_BLOCK = 64
CONV_W = 4
CONV_PAD_LEFT = 2
LRU_C = 8.0
GRID_W = 64
WIN_ROWS = 8
WIN_COLS = 16
D_FF = 2816
N_MOD = 9
EPS = 1e-6
NEG_INF = -1e30
LOG2_E = 1.4426950408889634
C_RG, C_Q, C_K, C_V = 512, 1024, 1536, 2048
D_IN = 2560

LANES = 128
SUBLANES = 8
VMEM_LIMIT_BYTES = 56 * 1024 * 1024

TOKEN_TILE = 512
OUT_TOKEN_TILE = 1024
FF_CHUNK = 256
MOD_ROWS = 16
REC_GROUP = LANES
GATE_CHUNK = 4096
SCAN_BLOCK = 64
SCAN_UNROLL = 8
ATT_Q_ROWS = 4
ATT_BAND_ROWS = 12
ATT_STEP_BLOCKS = 8
HEAD_PAIR = LANES // HEAD_DIM
ATT_ROW_CHUNK = 32


def _sigmoid(x):
    return 0.5 * (jnp.tanh(0.5 * x) + 1.0)


def _rms(x, g):
    return x * lax.rsqrt(jnp.mean(x * x, axis=-1, keepdims=True) + EPS) * g


def _dot(a, b):
    return jnp.dot(a, b, preferred_element_type=F32)


def _resident(shape):
    return pl.BlockSpec(shape, lambda *_: (0,) * len(shape), pipeline_mode=pl.Buffered(1))


def _mod_kernel(c_ref, w_ref, b_ref, o_ref):
    c = c_ref[...]
    s = (c * _sigmoid(c)).astype(BF16)
    o_ref[...] = _dot(s, w_ref[...].astype(BF16)) + b_ref[...]


def _modulation(c_rows, w_mod, b_mod):
    n = w_mod.shape[1]
    return pl.pallas_call(
        _mod_kernel,
        grid=(n // D_MODEL,),
        in_specs=[
            pl.BlockSpec((MOD_ROWS, D_MODEL), lambda j: (0, 0)),
            pl.BlockSpec((D_MODEL, D_MODEL), lambda j: (0, j)),
            pl.BlockSpec((1, D_MODEL), lambda j: (0, j)),
        ],
        out_specs=pl.BlockSpec((MOD_ROWS, D_MODEL), lambda j: (0, j)),
        out_shape=jax.ShapeDtypeStruct((MOD_ROWS, n), F32),
        name="mod",
    )(c_rows, w_mod, b_mod.reshape(1, n))


def _swiglu(h, wgu_ref, wd_ref):
    acc = None
    for c in range(D_FF // FF_CHUNK):
        lo = c * FF_CHUNK
        g = _dot(h, wgu_ref[:, lo:lo + FF_CHUNK])
        u = _dot(h, wgu_ref[:, D_FF + lo:D_FF + lo + FF_CHUNK])
        half = 0.5 * g
        a = (half * (jnp.tanh(half) + 1.0) * u).astype(BF16)
        d = _dot(a, wd_ref[lo:lo + FF_CHUNK, :])
        acc = d if acc is None else acc + d
    return acc


def _modulated_norm(x, npre_ref, mod_ref, idx):
    w = npre_ref[idx:idx + 1, :] * (1.0 + mod_ref[0, 3 * idx + 1:3 * idx + 2, :])
    return _rms(x, w) + mod_ref[0, 3 * idx:3 * idx + 1, :]


def _gated_residual(x, y, npost_ref, mod_ref, idx, res_w):
    w = (res_w * mod_ref[0, 3 * idx + 2:3 * idx + 3, :]) * npost_ref[idx:idx + 1, :]
    return x + _rms(y, w)


def _ffn_sublayer(x, mod_ref, npre_ref, npost_ref, idx, wgu_ref, wd_ref):
    h = _modulated_norm(x, npre_ref, mod_ref, idx)
    y = _swiglu(h.astype(BF16), wgu_ref, wd_ref)
    return _gated_residual(x, y, npost_ref, mod_ref, idx, 0.5)


def _layer_in_kernel(x_ref, mod_ref, npre_ref, npost_ref, wgu_ref, wd_ref, win_ref, *out_refs,
                     proj_cols, write_x):
    x1 = _ffn_sublayer(x_ref[0], mod_ref, npre_ref, npost_ref, 0, wgu_ref, wd_ref)
    outs = list(out_refs)
    if write_x:
        outs.pop(0)[0] = x1
    h = _modulated_norm(x1, npre_ref, mod_ref, 1).astype(BF16)
    for o_ref, (lo, width, mult) in zip(outs, proj_cols):
        p = _dot(h, win_ref[:, lo:lo + width])
        if mult != 1.0:
            p = p * mult
        o_ref[0] = p.astype(o_ref.dtype)


def _layer_in(x, mod, npre, npost, wgu, wd, win, proj, write_x):
    b, t, d = x.shape
    tm = min(TOKEN_TILE, t)
    assert t % tm == 0
    tok = lambda w: pl.BlockSpec((1, tm, w), lambda i, j: (i, j, 0))
    out_shape, out_specs = [], []
    if write_x:
        out_shape.append(jax.ShapeDtypeStruct((b, t, d), F32))
        out_specs.append(tok(d))
    for _, width, _, dtype in proj:
        out_shape.append(jax.ShapeDtypeStruct((b, t, width), dtype))
        out_specs.append(tok(width))
    return pl.pallas_call(
        functools.partial(_layer_in_kernel, proj_cols=tuple(p[:3] for p in proj), write_x=write_x),
        grid=(b, t // tm),
        in_specs=[
            tok(d),
            pl.BlockSpec((1, N_MOD, d), lambda i, j: (i, 0, 0)),
            _resident(npre.shape), _resident(npost.shape),
            _resident(wgu.shape), _resident(wd.shape), _resident(win.shape),
        ],
        out_specs=out_specs,
        out_shape=out_shape,
        compiler_params=pltpu.CompilerParams(
            dimension_semantics=("parallel", "parallel"), vmem_limit_bytes=VMEM_LIMIT_BYTES),
        name="layer_in",
    )(x, mod, npre, npost, wgu, wd, win)


def _layer_out_kernel(x_ref, yr_ref, ya_ref, mod_ref, npre_ref, npost_ref, wout_ref, wgu_ref, wd_ref,
                      o_ref):
    y = _dot(yr_ref[0], wout_ref[:D_REC, :]) + _dot(ya_ref[0], wout_ref[D_REC:, :])
    x2 = _gated_residual(x_ref[0], y, npost_ref, mod_ref, 1, 1.0)
    o_ref[0] = _ffn_sublayer(x2, mod_ref, npre_ref, npost_ref, 2, wgu_ref, wd_ref)


def _layer_out(x1, y_rec, y_att, mod, npre, npost, wout, wgu, wd):
    b, t, d = x1.shape
    tm = min(OUT_TOKEN_TILE, t)
    assert t % tm == 0
    tok = lambda w: pl.BlockSpec((1, tm, w), lambda i, j: (i, j, 0))
    return pl.pallas_call(
        _layer_out_kernel,
        grid=(b, t // tm),
        in_specs=[
            tok(d), tok(D_REC), tok(D_ATT),
            pl.BlockSpec((1, N_MOD, d), lambda i, j: (i, 0, 0)),
            _resident(npre.shape), _resident(npost.shape),
            _resident(wout.shape), _resident(wgu.shape), _resident(wd.shape),
        ],
        out_specs=tok(d),
        out_shape=jax.ShapeDtypeStruct((b, t, d), F32),
        compiler_params=pltpu.CompilerParams(
            dimension_semantics=("parallel", "parallel"), vmem_limit_bytes=VMEM_LIMIT_BYTES),
        name="layer_out",
    )(x1, y_rec, y_att, mod, npre, npost, wout, wgu, wd)


def _sublane_scan(a, u, row, reverse):
    for s in (1, 2, 4):
        if reverse:
            keep = row < SUBLANES - s
            shift = SUBLANES - s
        else:
            keep = row >= s
            shift = s
        a_sh = jnp.where(keep, pltpu.roll(a, shift, 0), 1.0)
        u_sh = jnp.where(keep, pltpu.roll(u, shift, 0), 0.0)
        u = a * u_sh + u
        a = a * a_sh
    return a, u


def _scan_block(a_ref, u_ref, h_ref, t0, carry, row, reverse):
    steps = range(SUBLANES - 1, -1, -1) if reverse else range(SUBLANES)
    h = p = None
    hs, ps = [None] * SUBLANES, [None] * SUBLANES
    for r in steps:
        a = a_ref[pl.ds(t0 + r, SUBLANES, stride=SUBLANES), :]
        u = u_ref[pl.ds(t0 + r, SUBLANES, stride=SUBLANES), :]
        h, p = (u, a) if h is None else (a * h + u, a * p)
        hs[r], ps[r] = h, p
    pp, hh = _sublane_scan(p, h, row, reverse)
    after = pp * carry + hh
    if reverse:
        before = jnp.where(row < SUBLANES - 1, pltpu.roll(after, SUBLANES - 1, 0), carry)
    else:
        before = jnp.where(row >= 1, pltpu.roll(after, 1, 0), carry)
    for r in range(SUBLANES):
        h_ref[pl.ds(t0 + r, SUBLANES, stride=SUBLANES), :] = hs[r] + ps[r] * before
    last = 0 if reverse else SUBLANES - 1
    return pp[last:last + 1, :] * carry + hh[last:last + 1, :]


def _rglru_kernel(xr_ref, xc_ref, gr_ref, cw_ref, cb_ref, wg_ref, bg_ref, lam_ref, o_ref,
                  pad_s, conv_s, af_s, uf_s, ab_s, ub_s, hf_s, hb_s):
    cg = REC_GROUP
    lam = lam_ref[0]
    log_sig_lam = jnp.minimum(lam, 0.0) - jnp.log(1.0 + jnp.exp(-jnp.abs(lam)))
    neg_half_c_lsl = (-0.5 * LRU_C) * log_sig_lam
    row = lax.broadcasted_iota(jnp.int32, (SUBLANES, cg), 0)

    def conv_gates(src_ref, n):
        halo = SUBLANES
        pad_s[0:halo, :] = jnp.zeros((halo, cg), F32)
        pad_s[halo:halo + n, :] = src_ref[0]
        pad_s[halo + n:2 * halo + n, :] = jnp.zeros((halo, cg), F32)
        y = cb_ref[...]
        for k in range(CONV_W):
            off = halo + k - CONV_PAD_LEFT
            y = y + pad_s[off:off + n, :] * cw_ref[k:k + 1, :]
        conv_s[0:n, :] = y

        chunk = min(GATE_CHUNK, n)

        def gate_body(ci, _):
            r0 = pl.multiple_of(ci * chunk, chunk)
            xv = conv_s[pl.ds(r0, chunk), :]
            g = _dot(xv.astype(BF16), wg_ref[0]) + bg_ref[0]
            x_half = 0.5 * xv
            for d, (a_s, u_s) in enumerate(((af_s, uf_s), (ab_s, ub_s))):
                ta = jnp.tanh(g[:, (2 * d) * cg:(2 * d + 1) * cg])
                tx = jnp.tanh(g[:, (2 * d + 1) * cg:(2 * d + 2) * cg])
                neg_log_a = neg_half_c_lsl[:, d * cg:(d + 1) * cg] * (ta + 1.0)
                a = jnp.exp2(neg_log_a * (-LOG2_E))
                a_s[pl.ds(r0, chunk), :] = a
                one_minus_a2 = jnp.tanh(neg_log_a) * (a * a + 1.0)
                root = jnp.where(one_minus_a2 > 0.0, one_minus_a2 * lax.rsqrt(one_minus_a2), 0.0)
                u_s[pl.ds(r0, chunk), :] = root * (tx + 1.0) * x_half
            return 0

        lax.fori_loop(0, n // chunk, gate_body, 0)

    def scan(n, hf0, hb0):
        blocks = n // SCAN_BLOCK

        def body(k, carry):
            hf, hb = carry
            tf = pl.multiple_of(k * SCAN_BLOCK, SCAN_BLOCK)
            tb = pl.multiple_of((blocks - 1 - k) * SCAN_BLOCK, SCAN_BLOCK)
            hf = _scan_block(af_s, uf_s, hf_s, tf, hf, row, False)
            hb = _scan_block(ab_s, ub_s, hb_s, tb, hb, row, True)
            return hf, hb

        return lax.fori_loop(0, blocks, body, (hf0, hb0), unroll=max(1, min(SCAN_UNROLL, blocks // 2)))

    n_ctx = xc_ref.shape[1]
    n_lat = xr_ref.shape[1]
    zero = jnp.zeros((1, cg), F32)
    conv_gates(xc_ref, n_ctx)
    hf, hb = scan(n_ctx, zero, zero)
    conv_gates(xr_ref, n_lat)
    scan(n_lat, hf, hb)
    y = hf_s[0:n_lat, :] + hb_s[0:n_lat, :]
    o_ref[0] = (y * jax.nn.gelu(gr_ref[0])).astype(o_ref.dtype)


def _rglru(xr, xr_ctx, gr, conv_w, conv_b, wg, bg, lam):
    b, t, _ = xr.shape
    n_ctx = xr_ctx.shape[1]
    cg = REC_GROUP
    seq = lambda n: pl.BlockSpec((1, n, cg), lambda i, j: (i, 0, j))
    grp = lambda shape: pl.BlockSpec((1,) + shape, lambda i, j: (j, 0, 0))
    return pl.pallas_call(
        _rglru_kernel,
        grid=(b, D_REC // cg),
        in_specs=[
            seq(t), seq(n_ctx), seq(t),
            pl.BlockSpec((CONV_W, cg), lambda i, j: (0, j)),
            pl.BlockSpec((1, cg), lambda i, j: (0, j)),
            grp((cg, 4 * cg)), grp((1, 4 * cg)), grp((1, 2 * cg)),
        ],
        out_specs=seq(t),
        out_shape=jax.ShapeDtypeStruct((b, t, D_REC), BF16),
        scratch_shapes=[pltpu.VMEM((t + 2 * SUBLANES, cg), F32)] + [pltpu.VMEM((t, cg), F32)] * 7,
        compiler_params=pltpu.CompilerParams(
            dimension_semantics=("parallel", "parallel"), vmem_limit_bytes=VMEM_LIMIT_BYTES),
        name="rglru",
    )(xr, xr_ctx, gr, conv_w, conv_b.reshape(1, D_REC), wg, bg, lam)


def _rglru_params(lru_wa, lru_ba, lru_wx, lru_bx, lru_lambda):
    cg = REC_GROUP
    n_grp = D_REC // cg
    per = cg // REC_BLOCK

    def block_diag(w):
        w = w.reshape(n_grp, per, REC_BLOCK, REC_BLOCK)
        eye = jnp.eye(per, dtype=w.dtype)
        return jnp.einsum("gpcd,pq->gpcqd", w, eye).reshape(n_grp, cg, cg)

    wg = (0.5 * jnp.concatenate([block_diag(lru_wa[0]), block_diag(lru_wx[0]),
                                 block_diag(lru_wa[1]), block_diag(lru_wx[1])], axis=-1)).astype(BF16)
    vec = lambda v: v.reshape(n_grp, 1, cg)
    bg = 0.5 * jnp.concatenate([vec(lru_ba[0]), vec(lru_bx[0]), vec(lru_ba[1]), vec(lru_bx[1])], axis=-1)
    lam = jnp.concatenate([vec(lru_lambda[0]), vec(lru_lambda[1])], axis=-1)
    return wg, bg, lam


def _attn_kernel(q_ref, k_ref, v_ref, kc_ref, vc_ref, tbl_ref, o_ref, s_buf, p_buf, *, rows):
    def block(j, _):
        _attn_block(pl.program_id(1) * ATT_STEP_BLOCKS + j, pl.multiple_of(j * ATT_Q_ROWS * GRID_W, GRID_W),
                    q_ref, k_ref, v_ref, kc_ref, vc_ref, tbl_ref, o_ref, s_buf, p_buf, rows)
        return 0

    lax.fori_loop(0, ATT_STEP_BLOCKS, block, 0)


def _attn_block(i, q_off, q_ref, k_ref, v_ref, kc_ref, vc_ref, tbl_ref, o_ref, s_buf, p_buf, rows):
    n_blk = rows // ATT_Q_ROWS
    start = jnp.clip(ATT_Q_ROWS * i - WIN_ROWS // 2, 0, rows - ATT_BAND_ROWS)
    kind = jnp.where(i == 0, 0, jnp.where(i == n_blk - 1, 2, 1))
    off = pl.multiple_of(start * GRID_W, GRID_W)
    band = ATT_BAND_ROWS * GRID_W
    tq = ATT_Q_ROWS * GRID_W
    lane = lax.broadcasted_iota(jnp.int32, (tq, LANES), 1)
    contract_last = (((1,), (1,)), ((), ()))

    def head_cols(head):
        pair = head // HEAD_PAIR
        return slice(pair * LANES, (pair + 1) * LANES)

    def scores(head):
        cols, hh, slot = head_cols(head), head % HEAD_PAIR, head % 2
        q = q_ref[0, pl.ds(q_off, tq), cols]
        mine = (lane >= hh * HEAD_DIM) & (lane < (hh + 1) * HEAD_DIM)
        qm = jnp.where(mine, q, jnp.zeros_like(q))
        s_loc = lax.dot_general(qm, k_ref[0, pl.ds(off, band), cols], contract_last,
                                preferred_element_type=F32)
        s_buf[slot, :, :band] = s_loc
        s_buf[slot, :, band:] = lax.dot_general(qm, kc_ref[0, :, cols], contract_last,
                                                preferred_element_type=F32)

    def softmax(head):
        slot = head % 2
        sums = []
        for r0 in range(0, tq, ATT_ROW_CHUNK):
            s = jnp.concatenate(
                [s_buf[slot, r0:r0 + ATT_ROW_CHUNK, :band] + tbl_ref[kind, head, r0:r0 + ATT_ROW_CHUNK, :],
                 s_buf[slot, r0:r0 + ATT_ROW_CHUNK, band:]], axis=-1)
            e =jnp.exp(s - jnp.max(s, axis=-1, keepdims=True))
            sums.append(jnp.sum(e, axis=-1, keepdims=True))
            p_buf[slot, r0:r0 + ATT_ROW_CHUNK, :] = e.astype(BF16)
        return jnp.concatenate(sums, axis=0)

    def values(head, denom):
        cols, slot = head_cols(head), head % 2
        o = (_dot(p_buf[slot, :, :band], v_ref[0, pl.ds(off, band), cols])
             + _dot(p_buf[slot, :, band:], vc_ref[0, :, cols]))
        return o / denom

    scores(0)
    outs = []
    for head in range(N_ATT_HEADS):
        if head + 1 < N_ATT_HEADS:
            scores(head + 1)
        outs.append(values(head, softmax(head)))
        if head % HEAD_PAIR == HEAD_PAIR - 1:
            pair_out = jnp.where(lane < HEAD_DIM, outs[-2], outs[-1])
            o_ref[0, pl.ds(q_off, tq), head_cols(head)] = pair_out.astype(o_ref.dtype)


def _attention(q, k, v, k_ctx, v_ctx, tbl):
    b, t, _ = q.shape
    n_ctx = k_ctx.shape[1]
    rows = t // GRID_W
    tq = ATT_Q_ROWS * GRID_W
    n_keys = ATT_BAND_ROWS * GRID_W + n_ctx
    full = lambda n: pl.BlockSpec((1, n, D_ATT), lambda bi, i: (bi, 0, 0))
    blk = pl.BlockSpec((1, ATT_STEP_BLOCKS * tq, D_ATT), lambda bi, i: (bi, i, 0))
    return pl.pallas_call(
        functools.partial(_attn_kernel, rows=rows),
        grid=(b, rows // (ATT_Q_ROWS * ATT_STEP_BLOCKS)),
        in_specs=[blk, full(t), full(t), full(n_ctx), full(n_ctx), _resident(tbl.shape)],
        out_specs=blk,
        out_shape=jax.ShapeDtypeStruct((b, t, D_ATT), BF16),
        scratch_shapes=[pltpu.VMEM((2, tq, n_keys), F32), pltpu.VMEM((2, tq, n_keys), BF16)],
        compiler_params=pltpu.CompilerParams(
            dimension_semantics=("parallel", "arbitrary"), vmem_limit_bytes=VMEM_LIMIT_BYTES),
        name="attn",
    )(q, k, v, k_ctx, v_ctx, tbl)


def _bias_tables(rpb, rows):
    n_blk = rows // ATT_Q_ROWS
    n_off = 2 * WIN_ROWS - 1
    pad = ATT_BAND_ROWS
    n_pairs = n_off + 2 * pad
    col = np.arange(GRID_W)
    col_off = np.clip(col[None, :] - col[:, None] + WIN_COLS - 1, 0, 2 * WIN_COLS - 2)
    n_col_off = 2 * WIN_COLS - 1
    rp = jnp.pad(rpb, ((0, 0), (pad, pad + 1), (0, 0)))
    z = jnp.concatenate([rp[:, :n_pairs], rp[:, 1:n_pairs + 1]], axis=-1)
    m = np.arange(2 * GRID_W)
    f = np.arange(2 * n_col_off)
    select = ((f[:, None, None] // n_col_off == m[None, None, :] // GRID_W)
              & (f[:, None, None] % n_col_off == col_off[:, m % GRID_W][None])).astype(np.float32)
    pairs = jnp.einsum("hRf,fqm->hRqm", z, select, precision=lax.Precision.HIGHEST)

    plan = []
    for i in (0, 1, n_blk - 1):
        band_start = int(np.clip(ATT_Q_ROWS * i - WIN_ROWS // 2, 0, rows - ATT_BAND_ROWS))
        per_row = []
        for rq in range(ATT_Q_ROWS):
            r = ATT_Q_ROWS * i + rq
            kr = band_start + np.arange(ATT_BAND_ROWS)
            win_start = int(np.clip(r - WIN_ROWS // 2, 0, rows - WIN_ROWS))
            row_valid = tuple(bool(ok) for ok in (kr >= win_start) & (kr < win_start + WIN_ROWS))
            first_off = band_start - r + WIN_ROWS - 1
            assert -pad <= first_off <= n_off
            per_row.append((first_off + pad, row_valid))
        plan.append(tuple(per_row))

    tq, band = ATT_Q_ROWS * GRID_W, ATT_BAND_ROWS * GRID_W
    return pl.pallas_call(
        functools.partial(_table_kernel, plan=tuple(plan)),
        grid=(N_ATT_HEADS,),
        in_specs=[pl.BlockSpec((1, n_pairs, GRID_W, 2 * GRID_W), lambda h: (h, 0, 0, 0))],
        out_specs=pl.BlockSpec((3, 1, tq, band), lambda h: (0, h, 0, 0)),
        out_shape=jax.ShapeDtypeStruct((3, N_ATT_HEADS, tq, band), F32),
        name="bias_table",
    )(pairs)


def _table_kernel(pairs_ref, o_ref, *, plan):
    qc = lax.broadcasted_iota(jnp.int32, (GRID_W, 2 * GRID_W), 0)
    m = lax.broadcasted_iota(jnp.int32, (GRID_W, 2 * GRID_W), 1)
    kc = jnp.where(m < GRID_W, m, m - GRID_W)
    col_start = jnp.clip(qc - WIN_COLS // 2, 0, GRID_W - WIN_COLS)
    col_ok = (kc >= col_start) & (kc < col_start + WIN_COLS)
    first_half = m < GRID_W
    masked = jnp.full((GRID_W, 2 * GRID_W), NEG_INF, F32)
    for kind, per_row in enumerate(plan):
        for rq, (lo, row_valid) in enumerate(per_row):
            for jp in range(ATT_BAND_ROWS // 2):
                ok0, ok1 = row_valid[2 * jp], row_valid[2 * jp + 1]
                if ok0 and ok1:
                    ok = col_ok
                elif ok0:
                    ok = col_ok & first_half
                elif ok1:
                    ok = col_ok & jnp.logical_not(first_half)
                else:
                    ok = None
                tile = masked if ok is None else jnp.where(ok, pairs_ref[0, lo + 2 * jp], NEG_INF)
                o_ref[kind, 0, rq * GRID_W:(rq + 1) * GRID_W, jp * 2 * GRID_W:(jp + 1) * 2 * GRID_W] = tile


def kernel(x, c, ctx, c_ctx, w_mod, b_mod, norm_pre, norm_post, ffn1_w_gu, ffn1_w_down, ffn2_w_gu,
           ffn2_w_down, w_in, w_out, conv_w, conv_b, lru_wa, lru_ba, lru_wx, lru_bx, lru_lambda, na_rpb):
    assert w_mod.shape[0] == 1, "single-layer problem"
    b, t, d = x.shape
    rows = t // GRID_W
    assert d == D_MODEL and t % (ATT_STEP_BLOCKS * ATT_Q_ROWS * GRID_W) == 0 and rows >= 3 * ATT_Q_ROWS
    assert b + 1 <= MOD_ROWS
    for n in (t, ctx.shape[1]):
        assert n % SCAN_BLOCK == 0 and n % min(GATE_CHUNK, n) == 0

    c_rows = jnp.concatenate([c, c_ctx[None], jnp.zeros((MOD_ROWS - b - 1, d), F32)], axis=0)
    mod_all = _modulation(c_rows, w_mod[0], b_mod[0])
    mod = mod_all[:b].reshape(b, N_MOD, d)
    mod_ctx = mod_all[b].reshape(1, N_MOD, d)

    npre, npost = norm_pre[0], norm_post[0]
    wgu1, wd1 = ffn1_w_gu[0].astype(BF16), ffn1_w_down[0].astype(BF16)
    wgu2, wd2 = ffn2_w_gu[0].astype(BF16), ffn2_w_down[0].astype(BF16)
    win, wout = w_in[0].astype(BF16), w_out[0].astype(BF16)

    q_scale = HEAD_DIM ** -0.5
    x1, xr, gr, q, k, v = _layer_in(
        x, mod, npre, npost, wgu1, wd1, win,
        proj=((0, D_REC, 1.0, F32), (C_RG, D_REC, 1.0, F32), (C_Q, D_ATT, q_scale, BF16),
              (C_K, D_ATT, 1.0, BF16), (C_V, D_ATT, 1.0, BF16)),
        write_x=True)
    n_ctx = ctx.shape[1]
    ctx_outs = _layer_in(
        ctx.reshape(1, b * n_ctx, d), mod_ctx, npre, npost, wgu1, wd1, win,
        proj=((0, D_REC, 1.0, F32), (C_K, D_ATT, 1.0, BF16), (C_V, D_ATT, 1.0, BF16)),
        write_x=False)
    xr_ctx, k_ctx, v_ctx = (o.reshape(b, n_ctx, o.shape[-1]) for o in ctx_outs)

    wg, bg, lam = _rglru_params(lru_wa[0], lru_ba[0], lru_wx[0], lru_bx[0], lru_lambda[0])
    y_rec = _rglru(xr, xr_ctx, gr, conv_w[0], conv_b[0], wg, bg, lam)
    y_att = _attention(q, k, v, k_ctx, v_ctx, _bias_tables(na_rpb[0].astype(F32), rows))
    return _layer_out(x1, y_rec, y_att, mod, npre, npost, wout, wgu2, wd2)
```
